```python
import jax, jax.numpy as jnp
from jax import lax
import numpy as np

D_MODEL = 1024
BATCH = 8
SEQ = 2048
DEPTH = 2

CHUNK = 64
N_MEM = 256

SB_HEADS = 8
SB_HEAD_DIM = 64
SB_WIDTH = SB_HEADS * SB_HEAD_DIM
SB_BLOCK = 128
CV_WIDTH = D_MODEL // 2
CV_KERNEL = 31
SC_WIDTH = D_MODEL // 2
SC_KERNEL = 3
N_BRANCH = 3
OFF_SB = 0
OFF_CV = OFF_SB + 3 * SB_WIDTH
OFF_SC = OFF_CV + 2 * CV_WIDTH
OFF_GATE = OFF_SC + 3 * SC_WIDTH
IN_WIDTH = OFF_GATE + N_BRANCH * D_MODEL
XA_HEADS = 4
XA_HEAD_DIM = D_MODEL // XA_HEADS
PEER_HEADS = 8
N_KEYS = 128
N_EXPERTS = N_KEYS * N_KEYS
PEER_KEY_DIM = 128
PEER_TOPK = 16
PEER_BLOCK = 128

kernel_name = "hybrid_sb_conformer_shortconv_peer_block"


def rmsnorm(x, g, eps=1e-6):
    xf = x.astype(jnp.float32)
    y = xf * lax.rsqrt(jnp.mean(xf * xf, axis=-1, keepdims=True) + eps)
    return (y * g.astype(jnp.float32)).astype(x.dtype)


def layernorm(x, g, b, eps=1e-5):
    xf = x.astype(jnp.float32)
    mu = jnp.mean(xf, axis=-1, keepdims=True)
    var = jnp.mean(jnp.square(xf - mu), axis=-1, keepdims=True)
    y = (xf - mu) * lax.rsqrt(var + eps)
    return (y * g.astype(jnp.float32) + b.astype(jnp.float32)).astype(x.dtype)


def causal_depthwise_conv(x, w):
    k, c = w.shape
    return lax.conv_general_dilated(
        x, w[:, None, :], window_strides=(1,), padding=[(k - 1, 0)],
        dimension_numbers=("NWC", "WIO", "NWC"), feature_group_count=c)


def stick_breaking_attention(q, k, v):
    q = q.transpose(0, 2, 1, 3)
    k = k.transpose(0, 2, 1, 3)
    v = v.transpose(0, 2, 1, 3)
    seq = q.shape[2]
    scale = SB_HEAD_DIM ** -0.5
    outs = []
    for i in range(seq // SB_BLOCK):
        lo, hi = i * SB_BLOCK, (i + 1) * SB_BLOCK
        qb, kb, vb = q[:, :, lo:hi], k[:, :, :hi], v[:, :, :hi]
        z = jnp.einsum("bhqd,bhkd->bhqk", qb, kb).astype(jnp.float32) * scale
        t_pos = lo + jnp.arange(SB_BLOCK)
        s_pos = jnp.arange(hi)
        mask = s_pos[None, :] < t_pos[:, None]
        log_beta = jax.nn.log_sigmoid(z)
        log_keep = jnp.where(mask, log_beta - z, 0.0)
        after = lax.cumsum(log_keep, axis=3, reverse=True) - log_keep
        a = jnp.where(mask, jnp.exp(log_beta + after), 0.0)
        outs.append(jnp.einsum("bhqk,bhkd->bhqd", a.astype(vb.dtype), vb))
    o = jnp.concatenate(outs, axis=2)
    return o.transpose(0, 2, 1, 3)


def hybrid_mixer(xn, w_in, conv_dw_w, conv_dw_b, conv_ln_g, conv_ln_b, sc_w, w_branch, b_gate, w_out):
    bsz, seq, _ = xn.shape
    p = xn @ w_in
    qkv = p[..., OFF_SB:OFF_CV].reshape(bsz, seq, 3, SB_HEADS, SB_HEAD_DIM)
    y_sb = stick_breaking_attention(qkv[:, :, 0], qkv[:, :, 1], qkv[:, :, 2]).reshape(bsz, seq, SB_WIDTH)
    cv_val, cv_gate = jnp.split(p[..., OFF_CV:OFF_SC], 2, axis=-1)
    u = cv_val * jax.nn.sigmoid(cv_gate)
    u = causal_depthwise_conv(u, conv_dw_w) + conv_dw_b
    y_cv = jax.nn.silu(layernorm(u, conv_ln_g, conv_ln_b))
    g_b, g_c, x_in = jnp.split(p[..., OFF_SC:OFF_GATE], 3, axis=-1)
    y_sc = g_b * causal_depthwise_conv(g_c * x_in, sc_w)
    branches = jnp.stack([y_sb, y_cv, y_sc], axis=2)
    proj = jnp.einsum("bsnc,ncd->bsnd", branches, w_branch)
    gates = jax.nn.sigmoid((p[..., OFF_GATE:] + b_gate).reshape(bsz, seq, N_BRANCH, D_MODEL))
    merged = jnp.sum(gates * proj, axis=2)
    return merged @ w_out


def memory_cross_attention(hn, mem, g_mem, wq, wkv, wo):
    bsz, seq, _ = hn.shape
    n_mem = mem.shape[1]
    q = (hn @ wq).reshape(bsz, seq, XA_HEADS, XA_HEAD_DIM)
    kv = (rmsnorm(mem, g_mem) @ wkv).reshape(bsz, n_mem, 2, XA_HEADS, XA_HEAD_DIM)
    k, v = kv[:, :, 0], kv[:, :, 1]
    s = jnp.einsum("bshd,bmhd->bhsm", q, k).astype(jnp.float32) * (XA_HEAD_DIM ** -0.5)
    a = jax.nn.softmax(s, axis=-1).astype(v.dtype)
    o = jnp.einsum("bhsm,bmhd->bshd", a, v).reshape(bsz, seq, D_MODEL)
    return o @ wo


def peer_ffn(hn, wq, sub_keys, u, v):
    bsz, seq, d = hn.shape
    q = (hn @ wq).reshape(bsz, seq, PEER_HEADS, 2, PEER_KEY_DIM)
    sc = jnp.einsum("bshpc,hpnc->bshpn", q, sub_keys).astype(jnp.float32)
    v1, i1 = lax.top_k(sc[..., 0, :], PEER_TOPK)
    v2, i2 = lax.top_k(sc[..., 1, :], PEER_TOPK)
    cand = (v1[..., :, None] + v2[..., None, :]).reshape(bsz, seq, PEER_HEADS, PEER_TOPK * PEER_TOPK)
    top, ci = lax.top_k(cand, PEER_TOPK)
    e1 = jnp.take_along_axis(i1, ci // PEER_TOPK, axis=-1)
    e2 = jnp.take_along_axis(i2, ci % PEER_TOPK, axis=-1)
    idx = e1 * N_KEYS + e2
    g = jax.nn.softmax(top, axis=-1).astype(hn.dtype)
    n_tok = bsz * seq
    n_sel = PEER_HEADS * PEER_TOPK
    xt = hn.reshape(n_tok // PEER_BLOCK, PEER_BLOCK, d)
    it = idx.reshape(n_tok // PEER_BLOCK, PEER_BLOCK, n_sel)
    gt = g.reshape(n_tok // PEER_BLOCK, PEER_BLOCK, n_sel)

    def block(args):
        xb, ib, gb = args
        a = jnp.einsum("td,ted->te", xb, u[ib])
        w = gb * jax.nn.gelu(a, approximate=False)
        return jnp.einsum("te,ted->td", w, v[ib])

    return lax.map(block, (xt, it, gt)).reshape(bsz, seq, d)


def setup_inputs(seed: int = 0) -> dict:
    key = jax.random.key(seed)
    ks = jax.random.split(key, 24)
    f32 = jnp.float32
    nrm = lambda k, shape, s: jax.random.normal(k, shape, f32) * s
    gain = lambda k, shape: 1.0 + 0.01 * jax.random.normal(k, shape, f32)
    L, D = DEPTH, D_MODEL
    return {
        "x": jax.random.normal(ks[0], (BATCH, SEQ, D), f32),
        "mem": jax.random.normal(ks[1], (BATCH, N_MEM, D), f32),
        "norm_mix_g": gain(ks[2], (L, D)),
        "w_in": nrm(ks[3], (L, D, IN_WIDTH), D ** -0.5),
        "conv_dw_w": nrm(ks[4], (L, CV_KERNEL, CV_WIDTH), CV_KERNEL ** -0.5),
        "conv_dw_b": nrm(ks[5], (L, CV_WIDTH), 0.01),
        "conv_ln_g": gain(ks[6], (L, CV_WIDTH)),
        "conv_ln_b": nrm(ks[7], (L, CV_WIDTH), 0.01),
        "sc_w": nrm(ks[8], (L, SC_KERNEL, SC_WIDTH), SC_KERNEL ** -0.5),
        "w_branch": nrm(ks[9], (L, N_BRANCH, SB_WIDTH, D), SB_WIDTH ** -0.5),
        "b_gate": nrm(ks[10], (L, N_BRANCH * D), 0.01),
        "w_out": nrm(ks[11], (L, D, D), D ** -0.5),
        "norm_xa_g": gain(ks[12], (L, D)),
        "norm_mem_g": gain(ks[13], (L, D)),
        "xa_wq": nrm(ks[14], (L, D, D), D ** -0.5),
        "xa_wkv": nrm(ks[15], (L, D, 2 * D), D ** -0.5),
        "xa_wo": nrm(ks[16], (L, D, D), D ** -0.5),
        "norm_ffn_g": gain(ks[17], (L, D)),
        "peer_wq": nrm(ks[18], (L, D, PEER_HEADS * 2 * PEER_KEY_DIM), D ** -0.5),
        "peer_keys": nrm(ks[19], (L, PEER_HEADS, 2, N_KEYS, PEER_KEY_DIM), PEER_KEY_DIM ** -0.5),
        "peer_u": nrm(ks[20], (L, N_EXPERTS, D), D ** -0.5),
        "peer_v": nrm(ks[21], (L, N_EXPERTS, D), PEER_HEADS ** -0.5),
        "final_g": gain(ks[22], (D,)),
    }


def reference(x, mem, norm_mix_g, w_in, conv_dw_w, conv_dw_b, conv_ln_g, conv_ln_b, sc_w,
              w_branch, b_gate, w_out, norm_xa_g, norm_mem_g, xa_wq, xa_wkv, xa_wo,
              norm_ffn_g, peer_wq, peer_keys, peer_u, peer_v, final_g):
    h = x
    for l in range(DEPTH):
        h = h + hybrid_mixer(rmsnorm(h, norm_mix_g[l]), w_in[l], conv_dw_w[l], conv_dw_b[l],
                             conv_ln_g[l], conv_ln_b[l], sc_w[l], w_branch[l], b_gate[l], w_out[l])
        h = h + memory_cross_attention(rmsnorm(h, norm_xa_g[l]), mem, norm_mem_g[l],
                                       xa_wq[l], xa_wkv[l], xa_wo[l])
        h = h + peer_ffn(rmsnorm(h, norm_ffn_g[l]), peer_wq[l], peer_keys[l], peer_u[l], peer_v[l])
    return rmsnorm(h, final_g)
```

```python
import functools

import jax
import jax.numpy as jnp
from jax import lax
from jax.experimental import pallas as pl
from jax.experimental.pallas import tpu as pltpu

F32 = jnp.float32
BF16 = jnp.bfloat16

LANES = 128
SUBLANES = 8
VMEM_LIMIT_BYTES = 56 * 1024 * 1024

SB_HEAD_DIM = 64
SB_HEADS_PER_TILE = LANES // SB_HEAD_DIM
XA_HEADS = 4
PEER_HEADS = 8
PEER_TOPK = 16
N_KEYS = 128
CONV_HALO = 32

_NT = (((1,), (1,)), ((), ()))


def _params(*sem):
    return pltpu.CompilerParams(dimension_semantics=sem, vmem_limit_bytes=VMEM_LIMIT_BYTES)


def _rms(x, g, eps=1e-6):
    return x * lax.rsqrt(jnp.mean(x * x, axis=-1, keepdims=True) + eps) * g


def _dot(a, b):
    return jnp.dot(a, b, preferred_element_type=F32)


def _norm_matmul_kernel(x_ref, g_ref, w_ref, o_ref, xn_ref):
    @pl.when(pl.program_id(1) == 0)
    def _():
        xn_ref[...] = _rms(x_ref[...], g_ref[...]).astype(BF16)

    o_ref[...] = _dot(xn_ref[...], w_ref[...]).astype(o_ref.dtype)


def _norm_matmul(x, g, w, *, col0, ncols, out_dtype, tm, tn, name):
    m, k = x.shape
    off = col0 // tn
    assert col0 % tn == 0 and ncols % tn == 0 and m % tm == 0
    return pl.pallas_call(
        _norm_matmul_kernel,
        grid=(m // tm, ncols // tn),
        in_specs=[
            pl.BlockSpec((tm, k), lambda i, j: (i, 0)),
            pl.BlockSpec((1, k), lambda i, j: (0, 0)),
            pl.BlockSpec((k, tn), lambda i, j: (0, j + off)),
        ],
        out_specs=pl.BlockSpec((tm, tn), lambda i, j: (i, j)),
        out_shape=jax.ShapeDtypeStruct((m, ncols), out_dtype),
        scratch_shapes=[pltpu.VMEM((tm, k), BF16)],
        compiler_params=_params("parallel", "arbitrary"),
        name=name,
    )(x, g.reshape(1, k), w)


def _softplus(z):
    return jnp.maximum(z, 0.0) + jnp.log(1.0 + jnp.exp(-jnp.abs(z)))


def _sb_kernel(q_ref, k_ref, v_ref, o_ref, *, blk, scale):
    i = pl.program_id(2)
    q = q_ref[...]
    lane = lax.broadcasted_iota(jnp.int32, (blk, LANES), 1)
    row = lax.broadcasted_iota(jnp.int32, (blk, blk), 0)
    col = lax.broadcasted_iota(jnp.int32, (blk, blk), 1)
    tri = jnp.where(row > col, 1.0, 0.0).astype(BF16)
    zero_q = jnp.zeros_like(q)
    outs = []
    for hh in range(SB_HEADS_PER_TILE):
        in_head = (lane // SB_HEAD_DIM) == hh
        qh = jnp.where(in_head, q, zero_q)

        def body(jj, carry, qh=qh):
            acc, c = carry
            j = i - jj
            start = pl.multiple_of(j * blk, blk)
            kj = k_ref[pl.ds(start, blk), :]
            vj = v_ref[pl.ds(start, blk), :]
            z = lax.dot_general(qh, kj, _NT, preferred_element_type=F32) * scale
            mask = (col + j * blk) < (row + i * blk)
            sp = _softplus(z)
            lk = jnp.where(mask, -sp, 0.0)
            lb = z - sp
            hi = lk.astype(BF16)
            lo = (lk - hi.astype(F32)).astype(BF16)
            after = _dot(hi, tri) + _dot(lo, tri) + c
            a = jnp.where(mask, jnp.exp(lb + after), 0.0).astype(BF16)
            acc = acc + _dot(a, vj)
            c = c + jnp.sum(lk, axis=1, keepdims=True)
            return acc, c

        acc, _ = lax.fori_loop(
            0, i + 1, body, (jnp.zeros((blk, LANES), F32), jnp.zeros((blk, 1), F32))
        )
        outs.append(acc)
    o = outs[0]
    for hh in range(1, SB_HEADS_PER_TILE):
        o = jnp.where((lane // SB_HEAD_DIM) == hh, outs[hh], o)
    o_ref[...] = o.astype(o_ref.dtype)


def _sb_attention(p_qkv, *, width, blk):
    b, s, _ = p_qkv.shape
    tiles = width // LANES
    blk = min(blk, s)
    kern = functools.partial(_sb_kernel, blk=blk, scale=SB_HEAD_DIM ** -0.5)
    return pl.pallas_call(
        kern,
        grid=(b, tiles, s // blk),
        in_specs=[
            pl.BlockSpec((None, blk, LANES), lambda bi, t, i: (bi, i, t)),
            pl.BlockSpec((None, s, LANES), lambda bi, t, i: (bi, 0, tiles + t)),
            pl.BlockSpec((None, s, LANES), lambda bi, t, i: (bi, 0, 2 * tiles + t)),
        ],
        out_specs=pl.BlockSpec((None, blk, LANES), lambda bi, t, i: (bi, i, t)),
        out_shape=jax.ShapeDtypeStruct((b, s, width), BF16),
        compiler_params=_params("parallel", "parallel", "arbitrary"),
        name="sb_attention",
    )(p_qkv, p_qkv, p_qkv)


def _conv_kernel(val_ref, gate_ref, gb_ref, gc_ref, xin_ref, dww_ref, dwb_ref, lng_ref, lnb_ref,
                 scw_ref, ycv_ref, ysc_ref, ubuf, sbuf, *, ts, rows):
    t = pl.program_id(1)
    c = ubuf.shape[1]

    @pl.when(t == 0)
    def _():
        ubuf[0:CONV_HALO, :] = jnp.zeros((CONV_HALO, c), F32)
        sbuf[0:CONV_HALO, :] = jnp.zeros((CONV_HALO, c), F32)

    @pl.when(t > 0)
    def _():
        ubuf[0:CONV_HALO, :] = ubuf[ts:ts + CONV_HALO, :]
        sbuf[0:CONV_HALO, :] = sbuf[ts:ts + CONV_HALO, :]

    ubuf[CONV_HALO:, :] = val_ref[...] * jax.nn.sigmoid(gate_ref[...])
    sbuf[CONV_HALO:, :] = gc_ref[...] * xin_ref[...]
    kw = dww_ref.shape[0]
    sw = scw_ref.shape[0]

    for r0 in range(0, ts, rows):
        acc = jnp.zeros((rows, c), F32)
        for j in range(kw):
            lo = r0 + CONV_HALO - (kw - 1) + j
            acc = acc + dww_ref[j:j + 1, :] * ubuf[lo:lo + rows, :]
        acc = acc + dwb_ref[...]
        mu = jnp.mean(acc, axis=-1, keepdims=True)
        d = acc - mu
        var = jnp.mean(d * d, axis=-1, keepdims=True)
        y = d * lax.rsqrt(var + 1e-5) * lng_ref[...] + lnb_ref[...]
        ycv_ref[r0:r0 + rows, :] = (y * jax.nn.sigmoid(y)).astype(ycv_ref.dtype)
        acc2 = jnp.zeros((rows, c), F32)
        for j in range(sw):
            lo = r0 + CONV_HALO - (sw - 1) + j
            acc2 = acc2 + scw_ref[j:j + 1, :] * sbuf[lo:lo + rows, :]
        ysc_ref[r0:r0 + rows, :] = (gb_ref[r0:r0 + rows, :] * acc2).astype(ysc_ref.dtype)


def _conv_branches(p_conv, dww, dwb, lng, lnb, scw, *, ts):
    b, s, w5 = p_conv.shape
    c = w5 // 5
    ts = min(ts, s)
    assert dww.shape[0] - 1 <= CONV_HALO and scw.shape[0] - 1 <= CONV_HALO
    kern = functools.partial(_conv_kernel, ts=ts, rows=64)
    col = lambda n: pl.BlockSpec((None, ts, c), lambda bi, t, n=n: (bi, t, n))
    full = lambda a: pl.BlockSpec(a.shape, lambda bi, t: (0, 0))
    vecs = [dww, dwb.reshape(1, c), lng.reshape(1, c), lnb.reshape(1, c), scw]
    out_spec = pl.BlockSpec((None, ts, c), lambda bi, t: (bi, t, 0))
    return pl.pallas_call(
        kern,
        grid=(b, s // ts),
        in_specs=[col(0), col(1), col(2), col(3), col(4)] + [full(a) for a in vecs],
        out_specs=[out_spec, out_spec],
        out_shape=[jax.ShapeDtypeStruct((b, s, c), BF16)] * 2,
        scratch_shapes=[pltpu.VMEM((CONV_HALO + ts, c), F32)] * 2,
        compiler_params=_params("parallel", "arbitrary"),
        name="conv_branches",
    )(p_conv, p_conv, p_conv, p_conv, p_conv, *vecs)


def _merge_kernel(ysb_ref, ycv_ref, ysc_ref, pg_ref, bg_ref, wb_ref, wo_ref, h_ref, o_ref):
    d = h_ref.shape[1]
    merged = jnp.zeros(h_ref.shape, F32)
    for n, y_ref in enumerate((ysb_ref, ycv_ref, ysc_ref)):
        proj = _dot(y_ref[...], wb_ref[n])
        gate = jax.nn.sigmoid(pg_ref[:, n * d:(n + 1) * d] + bg_ref[:, n * d:(n + 1) * d])
        merged = merged + gate * proj
    o_ref[...] = h_ref[...] + _dot(merged.astype(BF16), wo_ref[...])


def _merge(ysb, ycv, ysc, p_gate, b_gate, w_branch, w_out, h, *, tm):
    m, d = h.shape
    c = ysb.shape[1]
    nb = w_branch.shape[0]
    row = lambda w: pl.BlockSpec((tm, w), lambda i: (i, 0))
    return pl.pallas_call(
        _merge_kernel,
        grid=(m // tm,),
        in_specs=[
            row(c), row(c), row(c), row(nb * d),
            pl.BlockSpec((1, nb * d), lambda i: (0, 0)),
            pl.BlockSpec((nb, c, d), lambda i: (0, 0, 0)),
            pl.BlockSpec((d, d), lambda i: (0, 0)),
            row(d),
        ],
        out_specs=row(d),
        out_shape=jax.ShapeDtypeStruct((m, d), F32),
        compiler_params=_params("parallel"),
        name="merge",
    )(ysb, ycv, ysc, p_gate, b_gate.reshape(1, nb * d), w_branch, w_out, h)


def _xattn_kernel(h_ref, g_ref, wq_ref, kv_ref, wo_ref, o_ref):
    x = h_ref[...]
    d = x.shape[1]
    hd = d // XA_HEADS
    q = _dot(_rms(x, g_ref[...]).astype(BF16), wq_ref[...]).astype(BF16)
    outs = []
    for n in range(XA_HEADS):
        kh = kv_ref[:, n * hd:(n + 1) * hd]
        vh = kv_ref[:, d + n * hd:d + (n + 1) * hd]
        s = lax.dot_general(q[:, n * hd:(n + 1) * hd], kh, _NT, preferred_element_type=F32)
        s = s * (hd ** -0.5)
        e = jnp.exp(s - jnp.max(s, axis=-1, keepdims=True))
        a = e / jnp.sum(e, axis=-1, keepdims=True)
        outs.append(_dot(a.astype(BF16), vh).astype(BF16))
    o_ref[...] = x + _dot(jnp.concatenate(outs, axis=1), wo_ref[...])


def _cross_attention(h, g, wq, kv, wo, *, tm):
    b, s, d = h.shape
    n_mem = kv.shape[1]
    tm = min(tm, s)
    return pl.pallas_call(
        _xattn_kernel,
        grid=(b, s // tm),
        in_specs=[
            pl.BlockSpec((None, tm, d), lambda bi, i: (bi, i, 0)),
            pl.BlockSpec((1, d), lambda bi, i: (0, 0)),
            pl.BlockSpec((d, d), lambda bi, i: (0, 0)),
            pl.BlockSpec((None, n_mem, 2 * d), lambda bi, i: (bi, 0, 0)),
            pl.BlockSpec((d, d), lambda bi, i: (0, 0)),
        ],
        out_specs=pl.BlockSpec((None, tm, d), lambda bi, i: (bi, i, 0)),
        out_shape=jax.ShapeDtypeStruct((b, s, d), F32),
        compiler_params=_params("parallel", "parallel"),
        name="cross_attention",
    )(h, g.reshape(1, d), wq, kv, wo)


_KEY_POS = float(N_KEYS * N_KEYS)
_KEY_BIG = 2.0 ** 30


def _route_kernel(h_ref, g_ref, wq_ref, keys_ref, hn_ref, e1_ref, e2_ref, gate_ref,
                  q_scr, v_scr, i_scr, top_scr, key_scr, e1t_scr, e2t_scr, gt_scr, *, tm):
    k = PEER_TOPK
    hn = _rms(h_ref[...], g_ref[...]).astype(BF16)
    hn_ref[...] = hn
    q_scr[...] = _dot(hn, wq_ref[...]).astype(BF16)
    rowi = lax.broadcasted_iota(jnp.int32, (N_KEYS, tm), 0).astype(F32)

    def head(h, carry):
        for p in range(2):
            hp = h * 2 + p
            qs = q_scr[:, pl.ds(pl.multiple_of(hp * N_KEYS, N_KEYS), N_KEYS)]
            s = lax.dot_general(keys_ref[hp], qs, _NT, preferred_element_type=F32)

            def extract(i, s, p=p):
                m = jnp.max(s, axis=0, keepdims=True)
                idx = jnp.min(jnp.where(s == m, rowi, float(N_KEYS)), axis=0, keepdims=True)
                v_scr[p, pl.ds(i, 1), :] = m
                i_scr[p, pl.ds(i, 1), :] = idx
                return jnp.where(rowi == idx, -jnp.inf, s)

            lax.fori_loop(0, k, extract, s)
        v1, i1, v2, i2 = v_scr[0], i_scr[0], v_scr[1], i_scr[1]
        cands, keys = [], []

        def add_block(i_lo, i_n, j_n):
            jrow = lax.broadcasted_iota(jnp.int32, (j_n, tm), 0).astype(F32)
            for i in range(i_lo, i_lo + i_n):
                cands.append(v1[i:i + 1] + v2[0:j_n])
                pos = float(i * k) + jrow
                keys.append(pos * _KEY_POS + i1[i:i + 1] * float(N_KEYS) + i2[0:j_n])

        add_block(0, 2, k)
        add_block(2, 6, k // 2)
        cands.append(v1[k // 2:k] + v2[0:1])
        irow = lax.broadcasted_iota(jnp.int32, (k // 2, tm), 0).astype(F32) + float(k // 2)
        keys.append(irow * float(k) * _KEY_POS + i1[k // 2:k] * float(N_KEYS) + i2[0:1])
        cand = jnp.concatenate(cands, axis=0)
        key = jnp.concatenate(keys, axis=0)

        def pick(n, cand):
            m = jnp.max(cand, axis=0, keepdims=True)
            kmin = jnp.min(jnp.where(cand == m, key, _KEY_BIG), axis=0, keepdims=True)
            top_scr[pl.ds(n, 1), :] = m
            key_scr[pl.ds(n, 1), :] = kmin
            return jnp.where(key == kmin, -jnp.inf, cand)

        lax.fori_loop(0, k, pick, cand)
        top = top_scr[...]
        kk = key_scr[...]
        e = kk - jnp.floor(kk * (1.0 / _KEY_POS)) * _KEY_POS
        e1 = jnp.floor(e * (1.0 / N_KEYS))
        e2 = e - e1 * float(N_KEYS)
        ex = jnp.exp(top - top[0:1])
        gate = ex / jnp.sum(ex, axis=0, keepdims=True)
        r0 = pl.multiple_of(h * k, k)
        e1t_scr[pl.ds(r0, k), :] = e1
        e2t_scr[pl.ds(r0, k), :] = e2
        gt_scr[pl.ds(r0, k), :] = gate
        return carry

    lax.fori_loop(0, PEER_HEADS, head, 0)
    e1_ref[...] = e1t_scr[...].T.astype(jnp.int32)
    e2_ref[...] = e2t_scr[...].T.astype(jnp.int32)
    gate_ref[...] = gt_scr[...].T


def _peer_route(h, g, wq, keys, *, tm):
    m, d = h.shape
    qw = wq.shape[1]
    nsel = PEER_HEADS * PEER_TOPK
    k = PEER_TOPK
    kern = functools.partial(_route_kernel, tm=tm)
    row = lambda w: pl.BlockSpec((tm, w), lambda i: (i, 0))
    return pl.pallas_call(
        kern,
        grid=(m // tm,),
        in_specs=[
            row(d),
            pl.BlockSpec((1, d), lambda i: (0, 0)),
            pl.BlockSpec((d, qw), lambda i: (0, 0)),
            pl.BlockSpec(keys.shape, lambda i: (0, 0, 0)),
        ],
        out_specs=[row(d), row(nsel), row(nsel), row(nsel)],
        out_shape=[
            jax.ShapeDtypeStruct((m, d), BF16),
            jax.ShapeDtypeStruct((m, nsel), jnp.int32),
            jax.ShapeDtypeStruct((m, nsel), jnp.int32),
            jax.ShapeDtypeStruct((m, nsel), F32),
        ],
        scratch_shapes=[
            pltpu.VMEM((tm, qw), BF16),
            pltpu.VMEM((2, k, tm), F32),
            pltpu.VMEM((2, k, tm), F32),
            pltpu.VMEM((k, tm), F32),
            pltpu.VMEM((k, tm), F32),
            pltpu.VMEM((nsel, tm), F32),
            pltpu.VMEM((nsel, tm), F32),
            pltpu.VMEM((nsel, tm), F32),
        ],
        compiler_params=_params("parallel"),
        name="peer_route",
    )(h, g.reshape(1, d), wq, keys)


def _asel_kernel(hn_ref, ut_ref, e1_ref, e2_ref, o_ref, *, chunks):
    grp = pl.program_id(1)

    @pl.when(grp == 0)
    def _():
        o_ref[...] = jnp.zeros(o_ref.shape, F32)

    a = _dot(hn_ref[...], ut_ref[...])
    e1 = e1_ref[...]
    e2 = e2_ref[...]
    sel = o_ref[...]
    for c in range(chunks):
        gathered = jnp.take_along_axis(a[:, c * N_KEYS:(c + 1) * N_KEYS], e2, axis=1)
        sel = jnp.where(e1 == grp * chunks + c, gathered, sel)
    o_ref[...] = sel


def _peer_asel(hn, ut, e1, e2, *, tm, te):
    m, d = hn.shape
    ne = ut.shape[1]
    nsel = e1.shape[1]
    kern = functools.partial(_asel_kernel, chunks=te // N_KEYS)
    return pl.pallas_call(
        kern,
        grid=(m // tm, ne // te),
        in_specs=[
            pl.BlockSpec((tm, d), lambda i, j: (i, 0)),
            pl.BlockSpec((d, te), lambda i, j: (0, j)),
            pl.BlockSpec((tm, nsel), lambda i, j: (i, 0)),
            pl.BlockSpec((tm, nsel), lambda i, j: (i, 0)),
        ],
        out_specs=pl.BlockSpec((tm, nsel), lambda i, j: (i, 0)),
        out_shape=jax.ShapeDtypeStruct((m, nsel), F32),
        compiler_params=_params("parallel", "arbitrary"),
        name="peer_asel",
    )(hn, ut, e1, e2)


def _pout_kernel(h_ref, asel_ref, gate_ref, e1_ref, e2_ref, v_ref, o_ref, w_scr, w3_scr, *, tm, chunks):
    grp = pl.program_id(1)

    @pl.when(grp == 0)
    def _():
        a = asel_ref[...]
        gelu = 0.5 * a * (1.0 + lax.erf(a * (2.0 ** -0.5)))
        w_scr[...] = gate_ref[...] * gelu
        o_ref[...] = h_ref[...]
        sub = lax.broadcasted_iota(jnp.int32, (N_KEYS, e1_ref.shape[1]), 0)

        def token(t, carry):
            wrow = w_scr[pl.ds(t, 1), :]
            lhs = jnp.where(sub == e1_ref[pl.ds(t, 1), :], wrow, 0.0).astype(BF16)
            rhs = jnp.where(sub == e2_ref[pl.ds(t, 1), :], 1.0, 0.0).astype(BF16)
            g = lax.dot_general(lhs, rhs, _NT, preferred_element_type=F32)
            w3_scr[pl.ds(pl.multiple_of(t * N_KEYS, N_KEYS), N_KEYS), :] = g
            return carry

        lax.fori_loop(0, tm, token, 0)

    parts = [
        w3_scr[pl.ds(grp * chunks + c, tm, stride=N_KEYS), :].astype(BF16) for c in range(chunks)
    ]
    o_ref[...] += _dot(jnp.concatenate(parts, axis=1), v_ref[...])


def _peer_out(h, asel, gate, e1, e2, v, *, tm, te):
    m, d = h.shape
    ne = v.shape[0]
    nsel = e1.shape[1]
    kern = functools.partial(_pout_kernel, tm=tm, chunks=te // N_KEYS)
    sel = pl.BlockSpec((tm, nsel), lambda i, j: (i, 0))
    return pl.pallas_call(
        kern,
        grid=(m // tm, ne // te),
        in_specs=[
            pl.BlockSpec((tm, d), lambda i, j: (i, 0)),
            sel, sel, sel, sel,
            pl.BlockSpec((te, d), lambda i, j: (j, 0)),
        ],
        out_specs=pl.BlockSpec((tm, d), lambda i, j: (i, 0)),
        out_shape=jax.ShapeDtypeStruct((m, d), F32),
        scratch_shapes=[pltpu.VMEM((tm, nsel), F32), pltpu.VMEM((tm * N_KEYS, N_KEYS), F32)],
        compiler_params=_params("parallel", "arbitrary"),
        name="peer_out",
    )(h, asel, gate, e1, e2, v)


def _final_norm_kernel(x_ref, g_ref, o_ref):
    o_ref[...] = _rms(x_ref[...], g_ref[...])


def _final_norm(h, g, *, tm):
    m, d = h.shape
    return pl.pallas_call(
        _final_norm_kernel,
        grid=(m // tm,),
        in_specs=[pl.BlockSpec((tm, d), lambda i: (i, 0)), pl.BlockSpec((1, d), lambda i: (0, 0))],
        out_specs=pl.BlockSpec((tm, d), lambda i: (i, 0)),
        out_shape=jax.ShapeDtypeStruct((m, d), F32),
        compiler_params=_params("parallel"),
        name="final_norm",
    )(h, g.reshape(1, d))


def kernel(x, mem, norm_mix_g, w_in, conv_dw_w, conv_dw_b, conv_ln_g, conv_ln_b, sc_w, w_branch, b_gate, w_out, norm_xa_g, norm_mem_g, xa_wq, xa_wkv, xa_wo, norm_ffn_g, peer_wq, peer_keys, peer_u, peer_v, final_g):
    b, s, d = x.shape
    depth = w_in.shape[0]
    n_mem = mem.shape[1]
    sb_w = w_branch.shape[2]
    cv_w = conv_dw_w.shape[2]
    nb = w_branch.shape[1]
    qkv_w, conv_w, gate_w = 3 * sb_w, 5 * cv_w, nb * d
    m = b * s
    tm_big = min(1024, m)

    h = x.reshape(m, d)
    mem2 = mem.reshape(b * n_mem, d)
    for l in range(depth):
        w_in_l = w_in[l].astype(BF16)
        nm = functools.partial(_norm_matmul, h, norm_mix_g[l], w_in_l, tm=tm_big, tn=512)
        p_qkv = nm(col0=0, ncols=qkv_w, out_dtype=BF16, name="in_proj_qkv")
        p_conv = nm(col0=qkv_w, ncols=conv_w, out_dtype=F32, name="in_proj_conv")
        p_gate = nm(col0=qkv_w + conv_w, ncols=gate_w, out_dtype=F32, name="in_proj_gate")
        y_sb = _sb_attention(p_qkv.reshape(b, s, qkv_w), width=sb_w, blk=256)
        y_cv, y_sc = _conv_branches(p_conv.reshape(b, s, conv_w), conv_dw_w[l], conv_dw_b[l],
                                    conv_ln_g[l], conv_ln_b[l], sc_w[l], ts=256)
        h = _merge(y_sb.reshape(m, sb_w), y_cv.reshape(m, cv_w), y_sc.reshape(m, cv_w), p_gate,
                   b_gate[l], w_branch[l].astype(BF16), w_out[l].astype(BF16), h, tm=256)

        kv = _norm_matmul(mem2, norm_mem_g[l], xa_wkv[l].astype(BF16), col0=0, ncols=2 * d,
                          out_dtype=BF16, tm=min(1024, b * n_mem), tn=512, name="mem_kv")
        h = _cross_attention(h.reshape(b, s, d), norm_xa_g[l], xa_wq[l].astype(BF16),
                             kv.reshape(b, n_mem, 2 * d), xa_wo[l].astype(BF16), tm=256).reshape(m, d)

        keys = peer_keys[l].astype(BF16).reshape(PEER_HEADS * 2, N_KEYS, peer_keys.shape[-1])
        hn, e1, e2, gate = _peer_route(h, norm_ffn_g[l], peer_wq[l].astype(BF16), keys, tm=256)
        asel = _peer_asel(hn, peer_u[l].astype(BF16).T, e1, e2, tm=min(512, m), te=1024)
        h = _peer_out(h, asel, gate, e1, e2, peer_v[l].astype(BF16), tm=256, te=512)
    return _final_norm(h, final_g, tm=tm_big).reshape(b, s, d)
```

```python
import functools

import jax
import jax.numpy as jnp
from jax import lax
from jax.experimental import pallas as pl
from jax.experimental.pallas import tpu as pltpu

F32 = jnp.float32
BF16 = jnp.bfloat16

LANES = 128
SUBLANES = 8
MXU_WIDTH = 256
DOT_COLS = 2 * MXU_WIDTH
VMEM_LIMIT_BYTES = 56 * 1024 * 1024

SB_HEAD_DIM = 64
SB_HEADS_PER_TILE = LANES // SB_HEAD_DIM
XA_HEADS = 4
PEER_HEADS = 8
PEER_TOPK = 16
N_KEYS = 128
CONV_HALO = 32
W_GROUP = 16

_NT = (((1,), (1,)), ((), ()))


def _params(*sem):
    return pltpu.CompilerParams(dimension_semantics=sem, vmem_limit_bytes=VMEM_LIMIT_BYTES)


def _rms(x, g, eps=1e-6):
    return x * lax.rsqrt(jnp.mean(x * x, axis=-1, keepdims=True) + eps) * g


def _dot(a, b):
    return jnp.dot(a, b, preferred_element_type=F32)


def _norm_matmul_kernel(x_ref, g_ref, w_ref, o_ref, xn_ref):
    @pl.when(pl.program_id(1) == 0)
    def _():
        xn_ref[...] = _rms(x_ref[...], g_ref[...]).astype(BF16)

    o_ref[...] = _dot(xn_ref[...], w_ref[...]).astype(o_ref.dtype)


def _norm_matmul(x, g, w, *, col0, ncols, out_dtype, tm, tn, name):
    m, k = x.shape
    off = col0 // tn
    assert col0 % tn == 0 and ncols % tn == 0 and m % tm == 0
    return pl.pallas_call(
        _norm_matmul_kernel,
        grid=(m // tm, ncols // tn),
        in_specs=[
            pl.BlockSpec((tm, k), lambda i, j: (i, 0)),
            pl.BlockSpec((1, k), lambda i, j: (0, 0)),
            pl.BlockSpec((k, tn), lambda i, j: (0, j + off)),
        ],
        out_specs=pl.BlockSpec((tm, tn), lambda i, j: (i, j)),
        out_shape=jax.ShapeDtypeStruct((m, ncols), out_dtype),
        scratch_shapes=[pltpu.VMEM((tm, k), BF16)],
        compiler_params=_params("parallel", "arbitrary"),
        name=name,
    )(x, g.reshape(1, k), w)


def _softplus(z):
    return jnp.maximum(z, 0.0) + jnp.log(1.0 + jnp.exp(-jnp.abs(z)))


def _sb_kernel(q_ref, k_ref, v_ref, o_ref, *, blk, scale):
    i = pl.program_id(2)
    q = q_ref[...]
    lane = lax.broadcasted_iota(jnp.int32, (blk, LANES), 1)
    row = lax.broadcasted_iota(jnp.int32, (blk, blk), 0)
    col = lax.broadcasted_iota(jnp.int32, (blk, blk), 1)
    tri = jnp.where(row > col, 1.0, 0.0).astype(BF16)
    tri2 = jnp.concatenate([tri, tri], axis=0)
    zero_q = jnp.zeros_like(q)
    heads = range(SB_HEADS_PER_TILE)
    qs = q * jnp.asarray(scale, q.dtype)
    qhs = [jnp.where((lane // SB_HEAD_DIM) == hh, qs, zero_q) for hh in heads]

    def block(qh, j, c, diagonal):
        start = pl.multiple_of(j * blk, blk)
        kj = k_ref[pl.ds(start, blk), :]
        vj = v_ref[pl.ds(start, blk), :]
        z = lax.dot_general(qh, kj, _NT, preferred_element_type=F32)
        sp = _softplus(z)
        lk = jnp.where(col < row, -sp, 0.0) if diagonal else -sp
        lb = z - sp
        hi = lk.astype(BF16)
        lo = (lk - hi.astype(F32)).astype(BF16)
        after = _dot(jnp.concatenate([hi, lo], axis=1), tri2) + c
        a = jnp.exp(lb + after)
        if diagonal:
            a = jnp.where(col < row, a, 0.0)
        return _dot(a.astype(BF16), vj), jnp.sum(lk, axis=1, keepdims=True)

    def step(carry, js):
        out = []
        for qh, (acc, c) in zip(qhs, carry):
            for j in js:
                da, dc = block(qh, j, c, False)
                acc, c = acc + da, c + dc
            out.append((acc, c))
        return tuple(out)

    zero_c = jnp.zeros((blk, 1), F32)
    carry = tuple(block(qh, i, zero_c, True) for qh in qhs)
    carry = lax.fori_loop(
        0, i // 2, lambda n, carry: step(carry, (i - 1 - 2 * n, i - 2 - 2 * n)), carry)
    carry = lax.cond(i % 2 == 1, lambda carry: step(carry, (0,)), lambda carry: carry, carry)
    o = carry[0][0]
    for hh in heads[1:]:
        o = jnp.where((lane // SB_HEAD_DIM) == hh, carry[hh][0], o)
    o_ref[...] = o.astype(o_ref.dtype)


def _sb_attention(p_qkv, *, width, blk):
    b, s, _ = p_qkv.shape
    tiles = width // LANES
    blk = min(blk, s)
    kern = functools.partial(_sb_kernel, blk=blk, scale=SB_HEAD_DIM ** -0.5)
    return pl.pallas_call(
        kern,
        grid=(b, tiles, s // blk),
        in_specs=[
            pl.BlockSpec((None, blk, LANES), lambda bi, t, i: (bi, i, t)),
            pl.BlockSpec((None, s, LANES), lambda bi, t, i: (bi, 0, tiles + t)),
            pl.BlockSpec((None, s, LANES), lambda bi, t, i: (bi, 0, 2 * tiles + t)),
        ],
        out_specs=pl.BlockSpec((None, blk, LANES), lambda bi, t, i: (bi, i, t)),
        out_shape=jax.ShapeDtypeStruct((b, s, width), BF16),
        compiler_params=_params("parallel", "parallel", "arbitrary"),
        name="sb_attention",
    )(p_qkv, p_qkv, p_qkv)


def _conv_kernel(val_ref, gate_ref, gb_ref, gc_ref, xin_ref, dww_ref, dwb_ref, lng_ref, lnb_ref,
                 scw_ref, ycv_ref, ysc_ref, ubuf, sbuf, *, ts, rows):
    t = pl.program_id(1)
    c = ubuf.shape[1]

    @pl.when(t == 0)
    def _():
        ubuf[0:CONV_HALO, :] = jnp.zeros((CONV_HALO, c), F32)
        sbuf[0:CONV_HALO, :] = jnp.zeros((CONV_HALO, c), F32)

    @pl.when(t > 0)
    def _():
        ubuf[0:CONV_HALO, :] = ubuf[ts:ts + CONV_HALO, :]
        sbuf[0:CONV_HALO, :] = sbuf[ts:ts + CONV_HALO, :]

    ubuf[CONV_HALO:, :] = val_ref[...] * jax.nn.sigmoid(gate_ref[...])
    sbuf[CONV_HALO:, :] = gc_ref[...] * xin_ref[...]
    kw = dww_ref.shape[0]
    sw = scw_ref.shape[0]

    for r0 in range(0, ts, rows):
        acc = jnp.zeros((rows, c), F32)
        for j in range(kw):
            lo = r0 + CONV_HALO - (kw - 1) + j
            acc = acc + dww_ref[j:j + 1, :] * ubuf[lo:lo + rows, :]
        acc = acc + dwb_ref[...]
        mu = jnp.mean(acc, axis=-1, keepdims=True)
        d = acc - mu
        var = jnp.mean(d * d, axis=-1, keepdims=True)
        y = d * lax.rsqrt(var + 1e-5) * lng_ref[...] + lnb_ref[...]
        ycv_ref[r0:r0 + rows, :] = (y * jax.nn.sigmoid(y)).astype(ycv_ref.dtype)
        acc2 = jnp.zeros((rows, c), F32)
        for j in range(sw):
            lo = r0 + CONV_HALO - (sw - 1) + j
            acc2 = acc2 + scw_ref[j:j + 1, :] * sbuf[lo:lo + rows, :]
        ysc_ref[r0:r0 + rows, :] = (gb_ref[r0:r0 + rows, :] * acc2).astype(ysc_ref.dtype)


def _conv_branches(p_conv, dww, dwb, lng, lnb, scw, *, ts):
    b, s, w5 = p_conv.shape
    c = w5 // 5
    ts = min(ts, s)
    assert dww.shape[0] - 1 <= CONV_HALO and scw.shape[0] - 1 <= CONV_HALO
    kern = functools.partial(_conv_kernel, ts=ts, rows=64)
    col = lambda n: pl.BlockSpec((None, ts, c), lambda bi, t, n=n: (bi, t, n))
    full = lambda a: pl.BlockSpec(a.shape, lambda bi, t: (0, 0))
    vecs = [dww, dwb.reshape(1, c), lng.reshape(1, c), lnb.reshape(1, c), scw]
    out_spec = pl.BlockSpec((None, ts, c), lambda bi, t: (bi, t, 0))
    return pl.pallas_call(
        kern,
        grid=(b, s // ts),
        in_specs=[col(0), col(1), col(2), col(3), col(4)] + [full(a) for a in vecs],
        out_specs=[out_spec, out_spec],
        out_shape=[jax.ShapeDtypeStruct((b, s, c), BF16)] * 2,
        scratch_shapes=[pltpu.VMEM((CONV_HALO + ts, c), F32)] * 2,
        compiler_params=_params("parallel", "arbitrary"),
        name="conv_branches",
    )(p_conv, p_conv, p_conv, p_conv, p_conv, *vecs)


def _merge_kernel(ysb_ref, ycv_ref, ysc_ref, pg_ref, bg_ref, wb_ref, wo_ref, h_ref, o_ref):
    d = h_ref.shape[1]
    merged = jnp.zeros(h_ref.shape, F32)
    for n, y_ref in enumerate((ysb_ref, ycv_ref, ysc_ref)):
        proj = _dot(y_ref[...], wb_ref[n])
        gate = jax.nn.sigmoid(pg_ref[:, n * d:(n + 1) * d] + bg_ref[:, n * d:(n + 1) * d])
        merged = merged + gate * proj
    o_ref[...] = h_ref[...] + _dot(merged.astype(BF16), wo_ref[...])


def _merge(ysb, ycv, ysc, p_gate, b_gate, w_branch, w_out, h, *, tm):
    m, d = h.shape
    c = ysb.shape[1]
    nb = w_branch.shape[0]
    row = lambda w: pl.BlockSpec((tm, w), lambda i: (i, 0))
    return pl.pallas_call(
        _merge_kernel,
        grid=(m // tm,),
        in_specs=[
            row(c), row(c), row(c), row(nb * d),
            pl.BlockSpec((1, nb * d), lambda i: (0, 0)),
            pl.BlockSpec((nb, c, d), lambda i: (0, 0, 0)),
            pl.BlockSpec((d, d), lambda i: (0, 0)),
            row(d),
        ],
        out_specs=row(d),
        out_shape=jax.ShapeDtypeStruct((m, d), F32),
        compiler_params=_params("parallel"),
        name="merge",
    )(ysb, ycv, ysc, p_gate, b_gate.reshape(1, nb * d), w_branch, w_out, h)


def _xattn_kernel(h_ref, g_ref, wq_ref, kv_ref, wo_ref, o_ref):
    x = h_ref[...]
    d = x.shape[1]
    hd = d // XA_HEADS
    q = _dot(_rms(x, g_ref[...]).astype(BF16), wq_ref[...]).astype(BF16)
    outs = []
    for n in range(XA_HEADS):
        kh = kv_ref[:, n * hd:(n + 1) * hd]
        vh = kv_ref[:, d + n * hd:d + (n + 1) * hd]
        s = lax.dot_general(q[:, n * hd:(n + 1) * hd], kh, _NT, preferred_element_type=F32)
        s = s * (hd ** -0.5)
        e = jnp.exp(s - jnp.max(s, axis=-1, keepdims=True))
        a = e / jnp.sum(e, axis=-1, keepdims=True)
        outs.append(_dot(a.astype(BF16), vh).astype(BF16))
    o_ref[...] = x + _dot(jnp.concatenate(outs, axis=1), wo_ref[...])


def _cross_attention(h, g, wq, kv, wo, *, tm):
    b, s, d = h.shape
    n_mem = kv.shape[1]
    tm = min(tm, s)
    return pl.pallas_call(
        _xattn_kernel,
        grid=(b, s // tm),
        in_specs=[
            pl.BlockSpec((None, tm, d), lambda bi, i: (bi, i, 0)),
            pl.BlockSpec((1, d), lambda bi, i: (0, 0)),
            pl.BlockSpec((d, d), lambda bi, i: (0, 0)),
            pl.BlockSpec((None, n_mem, 2 * d), lambda bi, i: (bi, 0, 0)),
            pl.BlockSpec((d, d), lambda bi, i: (0, 0)),
        ],
        out_specs=pl.BlockSpec((None, tm, d), lambda bi, i: (bi, i, 0)),
        out_shape=jax.ShapeDtypeStruct((b, s, d), F32),
        compiler_params=_params("parallel", "parallel"),
        name="cross_attention",
    )(h, g.reshape(1, d), wq, kv, wo)


_KEY_POS = float(N_KEYS * N_KEYS)
_KEY_BIG = 2.0 ** 30


def _route_kernel(h_ref, g_ref, wq_ref, keys_ref, hn_ref, e1_ref, e2_ref, gate_ref,
                  q_scr, v_scr, i_scr, top_scr, key_scr, e1t_scr, e2t_scr, gt_scr, *, tm):
    k = PEER_TOPK
    hn = _rms(h_ref[...], g_ref[...]).astype(BF16)
    hn_ref[...] = hn
    q_scr[...] = _dot(hn, wq_ref[...]).astype(BF16)
    rowi = lax.broadcasted_iota(jnp.int32, (2, N_KEYS, tm), 1).astype(F32)

    def head(h, carry):
        halves = []
        for p in range(2):
            hp = h * 2 + p
            qs = q_scr[:, pl.ds(pl.multiple_of(hp * N_KEYS, N_KEYS), N_KEYS)]
            halves.append(lax.dot_general(keys_ref[hp], qs, _NT, preferred_element_type=F32))
        s = jnp.stack(halves, axis=0)

        def extract(i, s):
            m = jnp.max(s, axis=1, keepdims=True)
            idx = jnp.min(jnp.where(s == m, rowi, float(N_KEYS)), axis=1, keepdims=True)
            v_scr[:, pl.ds(i, 1), :] = m
            i_scr[:, pl.ds(i, 1), :] = idx
            return jnp.where(rowi == idx, -jnp.inf, s)

        lax.fori_loop(0, k, extract, s)
        v1, i1, v2, i2 = v_scr[0], i_scr[0], v_scr[1], i_scr[1]
        cands, keys = [], []

        def add_block(i_lo, i_n, j_n):
            jrow = lax.broadcasted_iota(jnp.int32, (j_n, tm), 0).astype(F32)
            for i in range(i_lo, i_lo + i_n):
                cands.append(v1[i:i + 1] + v2[0:j_n])
                pos = float(i * k) + jrow
                keys.append(pos * _KEY_POS + i1[i:i + 1] * float(N_KEYS) + i2[0:j_n])

        add_block(0, 2, k)
        add_block(2, 6, k // 2)
        cands.append(v1[k // 2:k] + v2[0:1])
        irow = lax.broadcasted_iota(jnp.int32, (k // 2, tm), 0).astype(F32) + float(k // 2)
        keys.append(irow * float(k) * _KEY_POS + i1[k // 2:k] * float(N_KEYS) + i2[0:1])
        cand = jnp.concatenate(cands, axis=0)
        key = jnp.concatenate(keys, axis=0)

        def pick(n, cand):
            m = jnp.max(cand, axis=0, keepdims=True)
            kmin = jnp.min(jnp.where(cand == m, key, _KEY_BIG), axis=0, keepdims=True)
            top_scr[pl.ds(n, 1), :] = m
            key_scr[pl.ds(n, 1), :] = kmin
            return jnp.where(key == kmin, -jnp.inf, cand)

        lax.fori_loop(0, k, pick, cand)
        top = top_scr[...]
        kk = key_scr[...]
        e = kk - jnp.floor(kk * (1.0 / _KEY_POS)) * _KEY_POS
        e1 = jnp.floor(e * (1.0 / N_KEYS))
        e2 = e - e1 * float(N_KEYS)
        ex = jnp.exp(top - top[0:1])
        gate = ex / jnp.sum(ex, axis=0, keepdims=True)
        r0 = pl.multiple_of(h * k, k)
        e1t_scr[pl.ds(r0, k), :] = e1
        e2t_scr[pl.ds(r0, k), :] = e2
        gt_scr[pl.ds(r0, k), :] = gate
        return carry

    lax.fori_loop(0, PEER_HEADS, head, 0)
    e1_ref[...] = e1t_scr[...].T.astype(jnp.int32)
    e2_ref[...] = e2t_scr[...].T.astype(jnp.int32)
    gate_ref[...] = gt_scr[...].T


def _peer_route(h, g, wq, keys, *, tm):
    m, d = h.shape
    qw = wq.shape[1]
    nsel = PEER_HEADS * PEER_TOPK
    k = PEER_TOPK
    kern = functools.partial(_route_kernel, tm=tm)
    row = lambda w: pl.BlockSpec((tm, w), lambda i: (i, 0))
    return pl.pallas_call(
        kern,
        grid=(m // tm,),
        in_specs=[
            row(d),
            pl.BlockSpec((1, d), lambda i: (0, 0)),
            pl.BlockSpec((d, qw), lambda i: (0, 0)),
            pl.BlockSpec(keys.shape, lambda i: (0, 0, 0)),
        ],
        out_specs=[row(d), row(nsel), row(nsel), row(nsel)],
        out_shape=[
            jax.ShapeDtypeStruct((m, d), BF16),
            jax.ShapeDtypeStruct((m, nsel), jnp.int32),
            jax.ShapeDtypeStruct((m, nsel), jnp.int32),
            jax.ShapeDtypeStruct((m, nsel), F32),
        ],
        scratch_shapes=[
            pltpu.VMEM((tm, qw), BF16),
            pltpu.VMEM((2, k, tm), F32),
            pltpu.VMEM((2, k, tm), F32),
            pltpu.VMEM((k, tm), F32),
            pltpu.VMEM((k, tm), F32),
            pltpu.VMEM((nsel, tm), F32),
            pltpu.VMEM((nsel, tm), F32),
            pltpu.VMEM((nsel, tm), F32),
        ],
        compiler_params=_params("parallel"),
        name="peer_route",
    )(h, g.reshape(1, d), wq, keys)


def _asel_kernel(hn_ref, ut_ref, e1_ref, e2_ref, o_ref, a0_scr, a1_scr, *, chunks):
    grp = pl.program_id(1)

    @pl.when(grp == 0)
    def _():
        o_ref[...] = jnp.zeros(o_ref.shape, F32)

    hn = hn_ref[...]
    e1 = e1_ref[...]
    e2 = e2_ref[...]
    per_dot = DOT_COLS // N_KEYS
    nblk = chunks // per_dot
    assert nblk % 2 == 0

    def matmul(n, dst):
        cols = pl.ds(pl.multiple_of(n * DOT_COLS, DOT_COLS), DOT_COLS)
        dst[...] = _dot(hn, ut_ref[:, cols])

    def gather(n, src, sel):
        for c in range(per_dot):
            gathered = jnp.take_along_axis(src[:, c * N_KEYS:(c + 1) * N_KEYS], e2, axis=1)
            sel = jnp.where(e1 == grp * chunks + n * per_dot + c, gathered, sel)
        return sel

    def body(m, sel):
        matmul(2 * m + 1, a1_scr)
        sel = gather(2 * m, a0_scr, sel)
        matmul(2 * m + 2, a0_scr)
        return gather(2 * m + 1, a1_scr, sel)

    matmul(0, a0_scr)
    sel = lax.fori_loop(0, nblk // 2 - 1, body, o_ref[...])
    matmul(nblk - 1, a1_scr)
    sel = gather(nblk - 2, a0_scr, sel)
    o_ref[...] = gather(nblk - 1, a1_scr, sel)


def _peer_asel(hn, ut, e1, e2, *, tm, te):
    m, d = hn.shape
    ne = ut.shape[1]
    nsel = e1.shape[1]
    kern = functools.partial(_asel_kernel, chunks=te // N_KEYS)
    return pl.pallas_call(
        kern,
        grid=(m // tm, ne // te),
        in_specs=[
            pl.BlockSpec((tm, d), lambda i, j: (i, 0)),
            pl.BlockSpec((d, te), lambda i, j: (0, j)),
            pl.BlockSpec((tm, nsel), lambda i, j: (i, 0)),
            pl.BlockSpec((tm, nsel), lambda i, j: (i, 0)),
        ],
        out_specs=pl.BlockSpec((tm, nsel), lambda i, j: (i, 0)),
        out_shape=jax.ShapeDtypeStruct((m, nsel), F32),
        scratch_shapes=[pltpu.VMEM((tm, DOT_COLS), F32)] * 2,
        compiler_params=_params("parallel", "arbitrary"),
        name="peer_asel",
    )(hn, ut, e1, e2)


def _pout_kernel(h_ref, asel_ref, gate_ref, e1_ref, e2_ref, v_ref, o_ref, w_scr, w3_scr, *, tm, chunks):
    grp = pl.program_id(1)

    @pl.when(grp == 0)
    def _():
        a = asel_ref[...]
        gelu = 0.5 * a * (1.0 + lax.erf(a * (2.0 ** -0.5)))
        w_scr[...] = gate_ref[...] * gelu
        o_ref[...] = h_ref[...]
        sub = lax.broadcasted_iota(jnp.int32, (N_KEYS, e1_ref.shape[1]), 0)

        def token_group(n, carry):
            t0 = pl.multiple_of(n * W_GROUP, W_GROUP)
            e1g = e1_ref[pl.ds(t0, W_GROUP), :]
            e2g = e2_ref[pl.ds(t0, W_GROUP), :]
            wg = w_scr[pl.ds(t0, W_GROUP), :]
            gs = []
            for t in range(W_GROUP):
                lhs = jnp.where(sub == e1g[t:t + 1], wg[t:t + 1], 0.0).astype(BF16)
                rhs = jnp.where(sub == e2g[t:t + 1], 1.0, 0.0).astype(BF16)
                gs.append(lax.dot_general(lhs, rhs, _NT, preferred_element_type=F32))
            g = jnp.swapaxes(jnp.stack(gs, axis=0), 0, 1)
            w3_scr[:, pl.ds(t0, W_GROUP), :] = g.astype(BF16)
            return carry

        lax.fori_loop(0, tm // W_GROUP, token_group, 0)

    parts = [w3_scr[grp * chunks + c] for c in range(chunks)]
    o_ref[...] += _dot(jnp.concatenate(parts, axis=1), v_ref[...])


def _peer_out(h, asel, gate, e1, e2, v, *, tm, te):
    m, d = h.shape
    ne = v.shape[0]
    nsel = e1.shape[1]
    kern = functools.partial(_pout_kernel, tm=tm, chunks=te // N_KEYS)
    sel = pl.BlockSpec((tm, nsel), lambda i, j: (i, 0))
    return pl.pallas_call(
        kern,
        grid=(m // tm, ne // te),
        in_specs=[
            pl.BlockSpec((tm, d), lambda i, j: (i, 0)),
            sel, sel, sel, sel,
            pl.BlockSpec((te, d), lambda i, j: (j, 0)),
        ],
        out_specs=pl.BlockSpec((tm, d), lambda i, j: (i, 0)),
        out_shape=jax.ShapeDtypeStruct((m, d), F32),
        scratch_shapes=[pltpu.VMEM((tm, nsel), F32), pltpu.VMEM((N_KEYS, tm, N_KEYS), BF16)],
        compiler_params=_params("parallel", "arbitrary"),
        name="peer_out",
    )(h, asel, gate, e1, e2, v)


def _final_norm_kernel(x_ref, g_ref, o_ref):
    o_ref[...] = _rms(x_ref[...], g_ref[...])


def _final_norm(h, g, *, tm):
    m, d = h.shape
    return pl.pallas_call(
        _final_norm_kernel,
        grid=(m // tm,),
        in_specs=[pl.BlockSpec((tm, d), lambda i: (i, 0)), pl.BlockSpec((1, d), lambda i: (0, 0))],
        out_specs=pl.BlockSpec((tm, d), lambda i: (i, 0)),
        out_shape=jax.ShapeDtypeStruct((m, d), F32),
        compiler_params=_params("parallel"),
        name="final_norm",
    )(h, g.reshape(1, d))


def kernel(x, mem, norm_mix_g, w_in, conv_dw_w, conv_dw_b, conv_ln_g, conv_ln_b, sc_w, w_branch, b_gate, w_out, norm_xa_g, norm_mem_g, xa_wq, xa_wkv, xa_wo, norm_ffn_g, peer_wq, peer_keys, peer_u, peer_v, final_g):
    b, s, d = x.shape
    depth = w_in.shape[0]
    n_mem = mem.shape[1]
    sb_w = w_branch.shape[2]
    cv_w = conv_dw_w.shape[2]
    nb = w_branch.shape[1]
    qkv_w, conv_w, gate_w = 3 * sb_w, 5 * cv_w, nb * d
    m = b * s
    tm_big = min(1024, m)

    h = x.reshape(m, d)
    mem2 = mem.reshape(b * n_mem, d)
    for l in range(depth):
        w_in_l = w_in[l].astype(BF16)
        nm = functools.partial(_norm_matmul, h, norm_mix_g[l], w_in_l, tm=tm_big, tn=512)
        p_qkv = nm(col0=0, ncols=qkv_w, out_dtype=BF16, name="in_proj_qkv")
        p_conv = nm(col0=qkv_w, ncols=conv_w, out_dtype=F32, name="in_proj_conv")
        p_gate = nm(col0=qkv_w + conv_w, ncols=gate_w, out_dtype=F32, name="in_proj_gate")
        y_sb = _sb_attention(p_qkv.reshape(b, s, qkv_w), width=sb_w, blk=256)
        y_cv, y_sc = _conv_branches(p_conv.reshape(b, s, conv_w), conv_dw_w[l], conv_dw_b[l],
                                    conv_ln_g[l], conv_ln_b[l], sc_w[l], ts=256)
        h = _merge(y_sb.reshape(m, sb_w), y_cv.reshape(m, cv_w), y_sc.reshape(m, cv_w), p_gate,
                   b_gate[l], w_branch[l].astype(BF16), w_out[l].astype(BF16), h, tm=256)

        kv = _norm_matmul(mem2, norm_mem_g[l], xa_wkv[l].astype(BF16), col0=0, ncols=2 * d,
                          out_dtype=BF16, tm=min(1024, b * n_mem), tn=512, name="mem_kv")
        h = _cross_attention(h.reshape(b, s, d), norm_xa_g[l], xa_wq[l].astype(BF16),
                             kv.reshape(b, n_mem, 2 * d), xa_wo[l].astype(BF16), tm=256).reshape(m, d)

        keys = peer_keys[l].astype(BF16).reshape(PEER_HEADS * 2, N_KEYS, peer_keys.shape[-1])
        hn, e1, e2, gate = _peer_route(h, norm_ffn_g[l], peer_wq[l].astype(BF16), keys, tm=256)
        asel = _peer_asel(hn, peer_u[l].astype(BF16).T, e1, e2, tm=min(512, m), te=4096)
        h = _peer_out(h, asel, gate, e1, e2, peer_v[l].astype(BF16), tm=min(512, m), te=1024)
    return _final_norm(h, final_g, tm=tm_big).reshape(b, s, d)
```

```python
import functools

import jax
import jax.numpy as jnp
from jax import lax
from jax.experimental import pallas as pl
from jax.experimental.pallas import tpu as pltpu

F32 = jnp.float32
BF16 = jnp.bfloat16

LANES = 128
SUBLANES = 8
MXU_WIDTH = 256
DOT_COLS = 2 * MXU_WIDTH
VMEM_LIMIT_BYTES = 56 * 1024 * 1024

SB_HEAD_DIM = 64
SB_HEADS_PER_TILE = LANES // SB_HEAD_DIM
XA_HEADS = 4
PEER_HEADS = 8
PEER_TOPK = 16
N_KEYS = 128
CONV_HALO = 32
W_GROUP = 16

_NT = (((1,), (1,)), ((), ()))


def _params(*sem):
    return pltpu.CompilerParams(dimension_semantics=sem, vmem_limit_bytes=VMEM_LIMIT_BYTES)


def _rms(x, g, eps=1e-6):
    return x * lax.rsqrt(jnp.mean(x * x, axis=-1, keepdims=True) + eps) * g


def _dot(a, b):
    return jnp.dot(a, b, preferred_element_type=F32)


def _norm_matmul_kernel(x_ref, g_ref, w_ref, o_ref, xn_ref):
    @pl.when(pl.program_id(1) == 0)
    def _():
        xn_ref[...] = _rms(x_ref[...], g_ref[...]).astype(BF16)

    o_ref[...] = _dot(xn_ref[...], w_ref[...]).astype(o_ref.dtype)


def _norm_matmul(x, g, w, *, col0, ncols, out_dtype, tm, tn, name):
    m, k = x.shape
    off = col0 // tn
    assert col0 % tn == 0 and ncols % tn == 0 and m % tm == 0
    return pl.pallas_call(
        _norm_matmul_kernel,
        grid=(m // tm, ncols // tn),
        in_specs=[
            pl.BlockSpec((tm, k), lambda i, j: (i, 0)),
            pl.BlockSpec((1, k), lambda i, j: (0, 0)),
            pl.BlockSpec((k, tn), lambda i, j: (0, j + off)),
        ],
        out_specs=pl.BlockSpec((tm, tn), lambda i, j: (i, j)),
        out_shape=jax.ShapeDtypeStruct((m, ncols), out_dtype),
        scratch_shapes=[pltpu.VMEM((tm, k), BF16)],
        compiler_params=_params("parallel", "arbitrary"),
        name=name,
    )(x, g.reshape(1, k), w)


def _softplus(z):
    return jnp.maximum(z, 0.0) + jnp.log(1.0 + jnp.exp(-jnp.abs(z)))


def _sb_kernel(q_ref, k_ref, v_ref, o_ref, *, bq, bk, scale):
    i = pl.program_id(2)
    per_q = bq // bk
    q = q_ref[...]
    lane = lax.broadcasted_iota(jnp.int32, (bq, LANES), 1)
    trow = lax.broadcasted_iota(jnp.int32, (bk, bk), 0)
    tcol = lax.broadcasted_iota(jnp.int32, (bk, bk), 1)
    tri = jnp.where(trow > tcol, 1.0, 0.0).astype(BF16)
    tri2 = jnp.concatenate([tri, tri], axis=0)
    zero_q = jnp.zeros_like(q)
    heads = range(SB_HEADS_PER_TILE)
    qs = q * jnp.asarray(scale, q.dtype)
    qhs = [jnp.where((lane // SB_HEAD_DIM) == hh, qs, zero_q) for hh in heads]

    def block(qh, j, c, causal):
        start = pl.multiple_of(j * bk, bk)
        kj = k_ref[pl.ds(start, bk), :]
        vj = v_ref[pl.ds(start, bk), :]
        z = lax.dot_general(qh, kj, _NT, preferred_element_type=F32)
        sp = _softplus(z)
        if causal:
            keep = (lax.broadcasted_iota(jnp.int32, z.shape, 1)
                    < lax.broadcasted_iota(jnp.int32, z.shape, 0))
        lk = jnp.where(keep, -sp, 0.0) if causal else -sp
        lb = z - sp
        hi = lk.astype(BF16)
        lo = (lk - hi.astype(F32)).astype(BF16)
        after = _dot(jnp.concatenate([hi, lo], axis=1), tri2) + c
        a = jnp.exp(lb + after)
        if causal:
            a = jnp.where(keep, a, 0.0)
        return _dot(a.astype(BF16), vj), jnp.sum(lk, axis=1, keepdims=True)

    def step(carry, js):
        out = []
        for qh, (acc, c) in zip(qhs, carry):
            for j in js:
                da, dc = block(qh, j, c, False)
                acc, c = acc + da, c + dc
            out.append((acc, c))
        return tuple(out)

    carry = []
    for qh in qhs:
        acc = jnp.zeros((bq, LANES), F32)
        c = jnp.zeros((bq, 1), F32)
        for d in reversed(range(per_q)):
            da, dc = block(qh[d * bk:], i * per_q + d, c[d * bk:], True)
            if d:
                da = jnp.concatenate([jnp.zeros((d * bk, LANES), F32), da], axis=0)
                dc = jnp.concatenate([jnp.zeros((d * bk, 1), F32), dc], axis=0)
            acc, c = acc + da, c + dc
        carry.append((acc, c))
    n_left = i * per_q
    carry = lax.fori_loop(
        0, n_left // 2,
        lambda n, carry: step(carry, (n_left - 1 - 2 * n, n_left - 2 - 2 * n)), tuple(carry))
    if per_q % 2:
        carry = lax.cond(n_left % 2 == 1, lambda carry: step(carry, (0,)), lambda carry: carry, carry)
    o = carry[0][0]
    for hh in heads[1:]:
        o = jnp.where((lane // SB_HEAD_DIM) == hh, carry[hh][0], o)
    o_ref[...] = o.astype(o_ref.dtype)


def _sb_attention(p_qkv, *, width, bq, bk):
    b, s, _ = p_qkv.shape
    tiles = width // LANES
    bq, bk = min(bq, s), min(bk, s)
    assert bq % bk == 0 and s % bq == 0
    kern = functools.partial(_sb_kernel, bq=bq, bk=bk, scale=SB_HEAD_DIM ** -0.5)
    return pl.pallas_call(
        kern,
        grid=(b, tiles, s // bq),
        in_specs=[
            pl.BlockSpec((None, bq, LANES), lambda bi, t, i: (bi, i, t)),
            pl.BlockSpec((None, s, LANES), lambda bi, t, i: (bi, 0, tiles + t)),
            pl.BlockSpec((None, s, LANES), lambda bi, t, i: (bi, 0, 2 * tiles + t)),
        ],
        out_specs=pl.BlockSpec((None, bq, LANES), lambda bi, t, i: (bi, i, t)),
        out_shape=jax.ShapeDtypeStruct((b, s, width), BF16),
        compiler_params=_params("parallel", "parallel", "arbitrary"),
        name="sb_attention",
    )(p_qkv, p_qkv, p_qkv)


def _conv_kernel(h_ref, g_ref, w_val, w_gate, w_gb, w_gc, w_xin, dww_ref, dwb_ref, lng_ref, lnb_ref,
                 scw_ref, ycv_ref, ysc_ref, ubuf, sbuf, gb_scr, *, ts, rows):
    t = pl.program_id(1)
    c = ubuf.shape[1]
    xn = _rms(h_ref[...], g_ref[...]).astype(BF16)

    @pl.when(t == 0)
    def _():
        ubuf[0:CONV_HALO, :] = jnp.zeros((CONV_HALO, c), F32)
        sbuf[0:CONV_HALO, :] = jnp.zeros((CONV_HALO, c), F32)

    @pl.when(t > 0)
    def _():
        ubuf[0:CONV_HALO, :] = ubuf[ts:ts + CONV_HALO, :]
        sbuf[0:CONV_HALO, :] = sbuf[ts:ts + CONV_HALO, :]

    ubuf[CONV_HALO:, :] = _dot(xn, w_val[...]) * jax.nn.sigmoid(_dot(xn, w_gate[...]))
    sbuf[CONV_HALO:, :] = _dot(xn, w_gc[...]) * _dot(xn, w_xin[...])
    gb_scr[...] = _dot(xn, w_gb[...])
    kw = dww_ref.shape[0]
    sw = scw_ref.shape[0]

    for r0 in range(0, ts, rows):
        acc = jnp.zeros((rows, c), F32)
        for j in range(kw):
            lo = r0 + CONV_HALO - (kw - 1) + j
            acc = acc + dww_ref[j:j + 1, :] * ubuf[lo:lo + rows, :]
        acc = acc + dwb_ref[...]
        mu = jnp.mean(acc, axis=-1, keepdims=True)
        d = acc - mu
        var = jnp.mean(d * d, axis=-1, keepdims=True)
        y = d * lax.rsqrt(var + 1e-5) * lng_ref[...] + lnb_ref[...]
        ycv_ref[r0:r0 + rows, :] = (y * jax.nn.sigmoid(y)).astype(ycv_ref.dtype)
        acc2 = jnp.zeros((rows, c), F32)
        for j in range(sw):
            lo = r0 + CONV_HALO - (sw - 1) + j
            acc2 = acc2 + scw_ref[j:j + 1, :] * sbuf[lo:lo + rows, :]
        ysc_ref[r0:r0 + rows, :] = (gb_scr[r0:r0 + rows, :] * acc2).astype(ysc_ref.dtype)


def _conv_branches(h, g, w_in, conv_col0, dww, dwb, lng, lnb, scw, *, ts):
    b, s, d = h.shape
    c = dww.shape[1]
    ts = min(ts, s)
    assert dww.shape[0] - 1 <= CONV_HALO and scw.shape[0] - 1 <= CONV_HALO and conv_col0 % c == 0
    kern = functools.partial(_conv_kernel, ts=ts, rows=64)
    w_col = lambda n: pl.BlockSpec((d, c), lambda bi, t, n=n: (0, conv_col0 // c + n))
    full = lambda a: pl.BlockSpec(a.shape, lambda bi, t: (0, 0))
    vecs = [dww, dwb.reshape(1, c), lng.reshape(1, c), lnb.reshape(1, c), scw]
    out_spec = pl.BlockSpec((None, ts, c), lambda bi, t: (bi, t, 0))
    return pl.pallas_call(
        kern,
        grid=(b, s // ts),
        in_specs=[pl.BlockSpec((None, ts, d), lambda bi, t: (bi, t, 0)),
                  pl.BlockSpec((1, d), lambda bi, t: (0, 0))]
        + [w_col(n) for n in range(5)] + [full(a) for a in vecs],
        out_specs=[out_spec, out_spec],
        out_shape=[jax.ShapeDtypeStruct((b, s, c), BF16)] * 2,
        scratch_shapes=[pltpu.VMEM((CONV_HALO + ts, c), F32)] * 2 + [pltpu.VMEM((ts, c), F32)],
        compiler_params=_params("parallel", "arbitrary"),
        name="conv_branches",
    )(h, g.reshape(1, d), w_in, w_in, w_in, w_in, w_in, *vecs)


def _merge_kernel(ysb_ref, ycv_ref, ysc_ref, h_ref, g_ref, wg0_ref, wg1_ref, wg2_ref, bg_ref,
                  wb_ref, wo_ref, o_ref):
    h = h_ref[...]
    d = h.shape[1]
    xn = _rms(h, g_ref[...]).astype(BF16)
    merged = jnp.zeros(h.shape, F32)
    branches = ((ysb_ref, wg0_ref), (ycv_ref, wg1_ref), (ysc_ref, wg2_ref))
    for n, (y_ref, wg_ref) in enumerate(branches):
        proj = _dot(y_ref[...], wb_ref[n])
        gate = jax.nn.sigmoid(_dot(xn, wg_ref[...]) + bg_ref[:, n * d:(n + 1) * d])
        merged = merged + gate * proj
    o_ref[...] = h + _dot(merged.astype(BF16), wo_ref[...])


def _merge(ysb, ycv, ysc, h, g, w_in, gate_col0, b_gate, w_branch, w_out, *, tm):
    m, d = h.shape
    c = ysb.shape[1]
    nb = w_branch.shape[0]
    assert nb == 3 and gate_col0 % d == 0
    row = lambda w: pl.BlockSpec((tm, w), lambda i: (i, 0))
    gate_w = lambda n: pl.BlockSpec((d, d), lambda i, n=n: (0, gate_col0 // d + n))
    return pl.pallas_call(
        _merge_kernel,
        grid=(m // tm,),
        in_specs=[
            row(c), row(c), row(c), row(d),
            pl.BlockSpec((1, d), lambda i: (0, 0)),
            gate_w(0), gate_w(1), gate_w(2),
            pl.BlockSpec((1, nb * d), lambda i: (0, 0)),
            pl.BlockSpec((nb, c, d), lambda i: (0, 0, 0)),
            pl.BlockSpec((d, d), lambda i: (0, 0)),
        ],
        out_specs=row(d),
        out_shape=jax.ShapeDtypeStruct((m, d), F32),
        compiler_params=_params("parallel"),
        name="merge",
    )(ysb, ycv, ysc, h, g.reshape(1, d), w_in, w_in, w_in, b_gate.reshape(1, nb * d), w_branch, w_out)


def _xattn_kernel(h_ref, g_ref, wq_ref, kv_ref, wo_ref, o_ref):
    x = h_ref[...]
    d = x.shape[1]
    hd = d // XA_HEADS
    q = _dot(_rms(x, g_ref[...]).astype(BF16), wq_ref[...]).astype(BF16)
    outs = []
    for n in range(XA_HEADS):
        kh = kv_ref[:, n * hd:(n + 1) * hd]
        vh = kv_ref[:, d + n * hd:d + (n + 1) * hd]
        s = lax.dot_general(q[:, n * hd:(n + 1) * hd], kh, _NT, preferred_element_type=F32)
        s = s * (hd ** -0.5)
        e = jnp.exp(s - jnp.max(s, axis=-1, keepdims=True))
        a = e / jnp.sum(e, axis=-1, keepdims=True)
        outs.append(_dot(a.astype(BF16), vh).astype(BF16))
    o_ref[...] = x + _dot(jnp.concatenate(outs, axis=1), wo_ref[...])


def _cross_attention(h, g, wq, kv, wo, *, tm):
    b, s, d = h.shape
    n_mem = kv.shape[1]
    tm = min(tm, s)
    return pl.pallas_call(
        _xattn_kernel,
        grid=(b, s // tm),
        in_specs=[
            pl.BlockSpec((None, tm, d), lambda bi, i: (bi, i, 0)),
            pl.BlockSpec((1, d), lambda bi, i: (0, 0)),
            pl.BlockSpec((d, d), lambda bi, i: (0, 0)),
            pl.BlockSpec((None, n_mem, 2 * d), lambda bi, i: (bi, 0, 0)),
            pl.BlockSpec((d, d), lambda bi, i: (0, 0)),
        ],
        out_specs=pl.BlockSpec((None, tm, d), lambda bi, i: (bi, i, 0)),
        out_shape=jax.ShapeDtypeStruct((b, s, d), F32),
        compiler_params=_params("parallel", "parallel"),
        name="cross_attention",
    )(h, g.reshape(1, d), wq, kv, wo)


_KEY_POS = float(N_KEYS * N_KEYS)
_KEY_BIG = 2.0 ** 30


def _route_kernel(h_ref, g_ref, wq_ref, keys_ref, hn_ref, e1_ref, e2_ref, gate_ref,
                  q_scr, v_scr, i_scr, top_scr, key_scr, e1t_scr, e2t_scr, gt_scr, *, tm):
    k = PEER_TOPK
    hn = _rms(h_ref[...], g_ref[...]).astype(BF16)
    hn_ref[...] = hn
    q_scr[...] = _dot(hn, wq_ref[...]).astype(BF16)
    rowi = lax.broadcasted_iota(jnp.int32, (2, N_KEYS, tm), 1).astype(F32)

    def head(h, carry):
        halves = []
        for p in range(2):
            hp = h * 2 + p
            qs = q_scr[:, pl.ds(pl.multiple_of(hp * N_KEYS, N_KEYS), N_KEYS)]
            halves.append(lax.dot_general(keys_ref[hp], qs, _NT, preferred_element_type=F32))
        s = jnp.stack(halves, axis=0)

        def extract(i, s):
            m = jnp.max(s, axis=1, keepdims=True)
            idx = jnp.min(jnp.where(s == m, rowi, float(N_KEYS)), axis=1, keepdims=True)
            v_scr[:, pl.ds(i, 1), :] = m
            i_scr[:, pl.ds(i, 1), :] = idx
            return jnp.where(rowi == idx, -jnp.inf, s)

        lax.fori_loop(0, k, extract, s)
        v1, i1, v2, i2 = v_scr[0], i_scr[0], v_scr[1], i_scr[1]
        cands, keys = [], []

        def add_block(i_lo, i_n, j_n):
            jrow = lax.broadcasted_iota(jnp.int32, (j_n, tm), 0).astype(F32)
            for i in range(i_lo, i_lo + i_n):
                cands.append(v1[i:i + 1] + v2[0:j_n])
                pos = float(i * k) + jrow
                keys.append(pos * _KEY_POS + i1[i:i + 1] * float(N_KEYS) + i2[0:j_n])

        add_block(0, 2, k)
        add_block(2, 6, k // 2)
        cands.append(v1[k // 2:k] + v2[0:1])
        irow = lax.broadcasted_iota(jnp.int32, (k // 2, tm), 0).astype(F32) + float(k // 2)
        keys.append(irow * float(k) * _KEY_POS + i1[k // 2:k] * float(N_KEYS) + i2[0:1])
        cand = jnp.concatenate(cands, axis=0)
        key = jnp.concatenate(keys, axis=0)

        def pick(n, cand):
            m = jnp.max(cand, axis=0, keepdims=True)
            kmin = jnp.min(jnp.where(cand == m, key, _KEY_BIG), axis=0, keepdims=True)
            top_scr[pl.ds(n, 1), :] = m
            key_scr[pl.ds(n, 1), :] = kmin
            return jnp.where(key == kmin, -jnp.inf, cand)

        lax.fori_loop(0, k, pick, cand)
        top = top_scr[...]
        kk = key_scr[...]
        e = kk - jnp.floor(kk * (1.0 / _KEY_POS)) * _KEY_POS
        e1 = jnp.floor(e * (1.0 / N_KEYS))
        e2 = e - e1 * float(N_KEYS)
        ex = jnp.exp(top - top[0:1])
        gate = ex / jnp.sum(ex, axis=0, keepdims=True)
        r0 = pl.multiple_of(h * k, k)
        e1t_scr[pl.ds(r0, k), :] = e1
        e2t_scr[pl.ds(r0, k), :] = e2
        gt_scr[pl.ds(r0, k), :] = gate
        return carry

    lax.fori_loop(0, PEER_HEADS, head, 0)
    e1_ref[...] = e1t_scr[...].T.astype(jnp.int32)
    e2_ref[...] = e2t_scr[...].T.astype(jnp.int32)
    gate_ref[...] = gt_scr[...].T


def _peer_route(h, g, wq, keys, *, tm):
    m, d = h.shape
    qw = wq.shape[1]
    nsel = PEER_HEADS * PEER_TOPK
    k = PEER_TOPK
    kern = functools.partial(_route_kernel, tm=tm)
    row = lambda w: pl.BlockSpec((tm, w), lambda i: (i, 0))
    return pl.pallas_call(
        kern,
        grid=(m // tm,),
        in_specs=[
            row(d),
            pl.BlockSpec((1, d), lambda i: (0, 0)),
            pl.BlockSpec((d, qw), lambda i: (0, 0)),
            pl.BlockSpec(keys.shape, lambda i: (0, 0, 0)),
        ],
        out_specs=[row(d), row(nsel), row(nsel), row(nsel)],
        out_shape=[
            jax.ShapeDtypeStruct((m, d), BF16),
            jax.ShapeDtypeStruct((m, nsel), jnp.int32),
            jax.ShapeDtypeStruct((m, nsel), jnp.int32),
            jax.ShapeDtypeStruct((m, nsel), F32),
        ],
        scratch_shapes=[
            pltpu.VMEM((tm, qw), BF16),
            pltpu.VMEM((2, k, tm), F32),
            pltpu.VMEM((2, k, tm), F32),
            pltpu.VMEM((k, tm), F32),
            pltpu.VMEM((k, tm), F32),
            pltpu.VMEM((nsel, tm), F32),
            pltpu.VMEM((nsel, tm), F32),
            pltpu.VMEM((nsel, tm), F32),
        ],
        compiler_params=_params("parallel"),
        name="peer_route",
    )(h, g.reshape(1, d), wq, keys)


def _asel_kernel(hn_ref, ut_ref, e1_ref, e2_ref, o_ref, a0_scr, a1_scr, *, chunks):
    grp = pl.program_id(1)

    @pl.when(grp == 0)
    def _():
        o_ref[...] = jnp.zeros(o_ref.shape, F32)

    hn = hn_ref[...]
    e1 = e1_ref[...]
    e2 = e2_ref[...]
    per_dot = DOT_COLS // N_KEYS
    nblk = chunks // per_dot
    assert nblk % 2 == 0

    def matmul(n, dst):
        cols = pl.ds(pl.multiple_of(n * DOT_COLS, DOT_COLS), DOT_COLS)
        dst[...] = _dot(hn, ut_ref[:, cols])

    def gather(n, src, sel):
        for c in range(per_dot):
            gathered = jnp.take_along_axis(src[:, c * N_KEYS:(c + 1) * N_KEYS], e2, axis=1)
            sel = jnp.where(e1 == grp * chunks + n * per_dot + c, gathered, sel)
        return sel

    def body(m, sel):
        matmul(2 * m + 1, a1_scr)
        sel = gather(2 * m, a0_scr, sel)
        matmul(2 * m + 2, a0_scr)
        return gather(2 * m + 1, a1_scr, sel)

    matmul(0, a0_scr)
    sel = lax.fori_loop(0, nblk // 2 - 1, body, o_ref[...])
    matmul(nblk - 1, a1_scr)
    sel = gather(nblk - 2, a0_scr, sel)
    o_ref[...] = gather(nblk - 1, a1_scr, sel)


def _peer_asel(hn, ut, e1, e2, *, tm, te):
    m, d = hn.shape
    ne = ut.shape[1]
    nsel = e1.shape[1]
    kern = functools.partial(_asel_kernel, chunks=te // N_KEYS)
    return pl.pallas_call(
        kern,
        grid=(m // tm, ne // te),
        in_specs=[
            pl.BlockSpec((tm, d), lambda i, j: (i, 0)),
            pl.BlockSpec((d, te), lambda i, j: (0, j)),
            pl.BlockSpec((tm, nsel), lambda i, j: (i, 0)),
            pl.BlockSpec((tm, nsel), lambda i, j: (i, 0)),
        ],
        out_specs=pl.BlockSpec((tm, nsel), lambda i, j: (i, 0)),
        out_shape=jax.ShapeDtypeStruct((m, nsel), F32),
        scratch_shapes=[pltpu.VMEM((tm, DOT_COLS), F32)] * 2,
        compiler_params=_params("parallel", "arbitrary"),
        name="peer_asel",
    )(hn, ut, e1, e2)


def _pout_kernel(h_ref, asel_ref, gate_ref, e1_ref, e2_ref, v_ref, o_ref, w_scr, w3_scr, *, tm, chunks):
    grp = pl.program_id(1)

    @pl.when(grp == 0)
    def _():
        a = asel_ref[...]
        gelu = 0.5 * a * (1.0 + lax.erf(a * (2.0 ** -0.5)))
        w_scr[...] = gate_ref[...] * gelu
        o_ref[...] = h_ref[...]
        sub = lax.broadcasted_iota(jnp.int32, (N_KEYS, e1_ref.shape[1]), 0)

        def token_group(n, carry):
            t0 = pl.multiple_of(n * W_GROUP, W_GROUP)
            e1g = e1_ref[pl.ds(t0, W_GROUP), :]
            e2g = e2_ref[pl.ds(t0, W_GROUP), :]
            wg = w_scr[pl.ds(t0, W_GROUP), :]
            gs = []
            for t in range(W_GROUP):
                lhs = jnp.where(sub == e1g[t:t + 1], wg[t:t + 1], 0.0).astype(BF16)
                rhs = jnp.where(sub == e2g[t:t + 1], 1.0, 0.0).astype(BF16)
                gs.append(lax.dot_general(lhs, rhs, _NT, preferred_element_type=F32))
            g = jnp.swapaxes(jnp.stack(gs, axis=0), 0, 1)
            w3_scr[:, pl.ds(t0, W_GROUP), :] = g.astype(BF16)
            return carry

        lax.fori_loop(0, tm // W_GROUP, token_group, 0)

    parts = [w3_scr[grp * chunks + c] for c in range(chunks)]
    o_ref[...] += _dot(jnp.concatenate(parts, axis=1), v_ref[...])


def _peer_out(h, asel, gate, e1, e2, v, *, tm, te):
    m, d = h.shape
    ne = v.shape[0]
    nsel = e1.shape[1]
    kern = functools.partial(_pout_kernel, tm=tm, chunks=te // N_KEYS)
    sel = pl.BlockSpec((tm, nsel), lambda i, j: (i, 0))
    return pl.pallas_call(
        kern,
        grid=(m // tm, ne // te),
        in_specs=[
            pl.BlockSpec((tm, d), lambda i, j: (i, 0)),
            sel, sel, sel, sel,
            pl.BlockSpec((te, d), lambda i, j: (j, 0)),
        ],
        out_specs=pl.BlockSpec((tm, d), lambda i, j: (i, 0)),
        out_shape=jax.ShapeDtypeStruct((m, d), F32),
        scratch_shapes=[pltpu.VMEM((tm, nsel), F32), pltpu.VMEM((N_KEYS, tm, N_KEYS), BF16)],
        compiler_params=_params("parallel", "arbitrary"),
        name="peer_out",
    )(h, asel, gate, e1, e2, v)


def _final_norm_kernel(x_ref, g_ref, o_ref):
    o_ref[...] = _rms(x_ref[...], g_ref[...])


def _final_norm(h, g, *, tm):
    m, d = h.shape
    return pl.pallas_call(
        _final_norm_kernel,
        grid=(m // tm,),
        in_specs=[pl.BlockSpec((tm, d), lambda i: (i, 0)), pl.BlockSpec((1, d), lambda i: (0, 0))],
        out_specs=pl.BlockSpec((tm, d), lambda i: (i, 0)),
        out_shape=jax.ShapeDtypeStruct((m, d), F32),
        compiler_params=_params("parallel"),
        name="final_norm",
    )(h, g.reshape(1, d))


def kernel(x, mem, norm_mix_g, w_in, conv_dw_w, conv_dw_b, conv_ln_g, conv_ln_b, sc_w, w_branch, b_gate, w_out, norm_xa_g, norm_mem_g, xa_wq, xa_wkv, xa_wo, norm_ffn_g, peer_wq, peer_keys, peer_u, peer_v, final_g):
    b, s, d = x.shape
    depth = w_in.shape[0]
    n_mem = mem.shape[1]
    sb_w = w_branch.shape[2]
    cv_w = conv_dw_w.shape[2]
    nb = w_branch.shape[1]
    qkv_w, conv_w = 3 * sb_w, 5 * cv_w
    m = b * s
    tm_big = min(1024, m)

    h = x.reshape(m, d)
    mem2 = mem.reshape(b * n_mem, d)
    for l in range(depth):
        w_in_l = w_in[l].astype(BF16)
        p_qkv = _norm_matmul(h, norm_mix_g[l], w_in_l, col0=0, ncols=qkv_w, out_dtype=BF16,
                             tm=tm_big, tn=512, name="in_proj_qkv")
        y_sb = _sb_attention(p_qkv.reshape(b, s, qkv_w), width=sb_w, bq=512, bk=256)
        y_cv, y_sc = _conv_branches(h.reshape(b, s, d), norm_mix_g[l], w_in_l, qkv_w, conv_dw_w[l],
                                    conv_dw_b[l], conv_ln_g[l], conv_ln_b[l], sc_w[l], ts=256)
        h = _merge(y_sb.reshape(m, sb_w), y_cv.reshape(m, cv_w), y_sc.reshape(m, cv_w), h,
                   norm_mix_g[l], w_in_l, qkv_w + conv_w, b_gate[l], w_branch[l].astype(BF16),
                   w_out[l].astype(BF16), tm=256)

        kv = _norm_matmul(mem2, norm_mem_g[l], xa_wkv[l].astype(BF16), col0=0, ncols=2 * d,
                          out_dtype=BF16, tm=min(1024, b * n_mem), tn=512, name="mem_kv")
        h = _cross_attention(h.reshape(b, s, d), norm_xa_g[l], xa_wq[l].astype(BF16),
                             kv.reshape(b, n_mem, 2 * d), xa_wo[l].astype(BF16), tm=256).reshape(m, d)

        keys = peer_keys[l].astype(BF16).reshape(PEER_HEADS * 2, N_KEYS, peer_keys.shape[-1])
        hn, e1, e2, gate = _peer_route(h, norm_ffn_g[l], peer_wq[l].astype(BF16), keys, tm=256)
        asel = _peer_asel(hn, peer_u[l].astype(BF16).T, e1, e2, tm=min(512, m), te=4096)
        h = _peer_out(h, asel, gate, e1, e2, peer_v[l].astype(BF16), tm=min(512, m), te=1024)
    return _final_norm(h, final_g, tm=tm_big).reshape(b, s, d)
```

```python
import functools

import jax
import jax.numpy as jnp
from jax import lax
from jax.experimental import pallas as pl
from jax.experimental.pallas import tpu as pltpu

F32 = jnp.float32
BF16 = jnp.bfloat16

LANES = 128
SUBLANES = 8
MXU_WIDTH = 256
DOT_COLS = 2 * MXU_WIDTH
VMEM_LIMIT_BYTES = 56 * 1024 * 1024

SB_HEAD_DIM = 64
SB_HEADS_PER_TILE = LANES // SB_HEAD_DIM
XA_HEADS = 4
PEER_HEADS = 8
PEER_TOPK = 16
N_KEYS = 128
CONV_HALO = 32
W_GROUP = 16

_NT = (((1,), (1,)), ((), ()))


def _params(*sem):
    return pltpu.CompilerParams(dimension_semantics=sem, vmem_limit_bytes=VMEM_LIMIT_BYTES)


def _rms(x, g, eps=1e-6):
    return x * lax.rsqrt(jnp.mean(x * x, axis=-1, keepdims=True) + eps) * g


def _dot(a, b):
    return jnp.dot(a, b, preferred_element_type=F32)


def _norm_matmul_kernel(x_ref, g_ref, w_ref, o_ref, xn_ref):
    @pl.when(pl.program_id(1) == 0)
    def _():
        xn_ref[...] = _rms(x_ref[...], g_ref[...]).astype(BF16)

    o_ref[...] = _dot(xn_ref[...], w_ref[...]).astype(o_ref.dtype)


def _norm_matmul(x, g, w, *, col0, ncols, out_dtype, tm, tn, name):
    m, k = x.shape
    off = col0 // tn
    assert col0 % tn == 0 and ncols % tn == 0 and m % tm == 0
    return pl.pallas_call(
        _norm_matmul_kernel,
        grid=(m // tm, ncols // tn),
        in_specs=[
            pl.BlockSpec((tm, k), lambda i, j: (i, 0)),
            pl.BlockSpec((1, k), lambda i, j: (0, 0)),
            pl.BlockSpec((k, tn), lambda i, j: (0, j + off)),
        ],
        out_specs=pl.BlockSpec((tm, tn), lambda i, j: (i, j)),
        out_shape=jax.ShapeDtypeStruct((m, ncols), out_dtype),
        scratch_shapes=[pltpu.VMEM((tm, k), BF16)],
        compiler_params=_params("parallel", "arbitrary"),
        name=name,
    )(x, g.reshape(1, k), w)


def _softplus(z):
    return jnp.maximum(z, 0.0) + jnp.log(1.0 + jnp.exp(-jnp.abs(z)))


def _sb_kernel(q_ref, k_ref, v_ref, o_ref, *, bq, bk, scale):
    i = pl.program_id(2)
    per_q = bq // bk
    q = q_ref[...]
    lane = lax.broadcasted_iota(jnp.int32, (bq, LANES), 1)
    trow = lax.broadcasted_iota(jnp.int32, (bk, bk), 0)
    tcol = lax.broadcasted_iota(jnp.int32, (bk, bk), 1)
    tri = jnp.where(trow > tcol, 1.0, 0.0).astype(BF16)
    tri2 = jnp.concatenate([tri, tri], axis=0)
    zero_q = jnp.zeros_like(q)
    heads = range(SB_HEADS_PER_TILE)
    qs = q * jnp.asarray(scale, q.dtype)
    qhs = [jnp.where((lane // SB_HEAD_DIM) == hh, qs, zero_q) for hh in heads]

    def block(qh, j, c, causal):
        start = pl.multiple_of(j * bk, bk)
        kj = k_ref[pl.ds(start, bk), :]
        vj = v_ref[pl.ds(start, bk), :]
        z = lax.dot_general(qh, kj, _NT, preferred_element_type=F32)
        sp = _softplus(z)
        if causal:
            keep = (lax.broadcasted_iota(jnp.int32, z.shape, 1)
                    < lax.broadcasted_iota(jnp.int32, z.shape, 0))
        lk = jnp.where(keep, -sp, 0.0) if causal else -sp
        lb = z - sp
        hi = lk.astype(BF16)
        lo = (lk - hi.astype(F32)).astype(BF16)
        after = _dot(jnp.concatenate([hi, lo], axis=1), tri2) + c
        a = jnp.exp(lb + after)
        if causal:
            a = jnp.where(keep, a, 0.0)
        return _dot(a.astype(BF16), vj), jnp.sum(lk, axis=1, keepdims=True)

    def step(carry, js):
        out = []
        for qh, (acc, c) in zip(qhs, carry):
            for j in js:
                da, dc = block(qh, j, c, False)
                acc, c = acc + da, c + dc
            out.append((acc, c))
        return tuple(out)

    carry = []
    for qh in qhs:
        acc = jnp.zeros((bq, LANES), F32)
        c = jnp.zeros((bq, 1), F32)
        for d in reversed(range(per_q)):
            da, dc = block(qh[d * bk:], i * per_q + d, c[d * bk:], True)
            if d:
                da = jnp.concatenate([jnp.zeros((d * bk, LANES), F32), da], axis=0)
                dc = jnp.concatenate([jnp.zeros((d * bk, 1), F32), dc], axis=0)
            acc, c = acc + da, c + dc
        carry.append((acc, c))
    n_left = i * per_q
    carry = lax.fori_loop(
        0, n_left // 2,
        lambda n, carry: step(carry, (n_left - 1 - 2 * n, n_left - 2 - 2 * n)), tuple(carry))
    if per_q % 2:
        carry = lax.cond(n_left % 2 == 1, lambda carry: step(carry, (0,)), lambda carry: carry, carry)
    o = carry[0][0]
    for hh in heads[1:]:
        o = jnp.where((lane // SB_HEAD_DIM) == hh, carry[hh][0], o)
    o_ref[...] = o.astype(o_ref.dtype)


def _sb_attention(p_qkv, *, width, bq, bk):
    b, s, _ = p_qkv.shape
    tiles = width // LANES
    bq, bk = min(bq, s), min(bk, s)
    assert bq % bk == 0 and s % bq == 0
    kern = functools.partial(_sb_kernel, bq=bq, bk=bk, scale=SB_HEAD_DIM ** -0.5)
    return pl.pallas_call(
        kern,
        grid=(b, tiles, s // bq),
        in_specs=[
            pl.BlockSpec((None, bq, LANES), lambda bi, t, i: (bi, i, t)),
            pl.BlockSpec((None, s, LANES), lambda bi, t, i: (bi, 0, tiles + t)),
            pl.BlockSpec((None, s, LANES), lambda bi, t, i: (bi, 0, 2 * tiles + t)),
        ],
        out_specs=pl.BlockSpec((None, bq, LANES), lambda bi, t, i: (bi, i, t)),
        out_shape=jax.ShapeDtypeStruct((b, s, width), BF16),
        compiler_params=_params("parallel", "parallel", "arbitrary"),
        name="sb_attention",
    )(p_qkv, p_qkv, p_qkv)


def _conv_kernel(h_ref, g_ref, w_val, w_gate, w_gb, w_gc, w_xin, dww_ref, dwb_ref, lng_ref, lnb_ref,
                 scw_ref, ycv_ref, ysc_ref, ubuf, sbuf, gb_scr, *, ts, rows):
    t = pl.program_id(1)
    c = ubuf.shape[1]
    xn = _rms(h_ref[...], g_ref[...]).astype(BF16)

    @pl.when(t == 0)
    def _():
        ubuf[0:CONV_HALO, :] = jnp.zeros((CONV_HALO, c), F32)
        sbuf[0:CONV_HALO, :] = jnp.zeros((CONV_HALO, c), F32)

    @pl.when(t > 0)
    def _():
        ubuf[0:CONV_HALO, :] = ubuf[ts:ts + CONV_HALO, :]
        sbuf[0:CONV_HALO, :] = sbuf[ts:ts + CONV_HALO, :]

    ubuf[CONV_HALO:, :] = _dot(xn, w_val[...]) * jax.nn.sigmoid(_dot(xn, w_gate[...]))
    sbuf[CONV_HALO:, :] = _dot(xn, w_gc[...]) * _dot(xn, w_xin[...])
    gb_scr[...] = _dot(xn, w_gb[...])
    kw = dww_ref.shape[0]
    sw = scw_ref.shape[0]

    for r0 in range(0, ts, rows):
        acc = jnp.zeros((rows, c), F32)
        for j in range(kw):
            lo = r0 + CONV_HALO - (kw - 1) + j
            acc = acc + dww_ref[j:j + 1, :] * ubuf[lo:lo + rows, :]
        acc = acc + dwb_ref[...]
        mu = jnp.mean(acc, axis=-1, keepdims=True)
        d = acc - mu
        var = jnp.mean(d * d, axis=-1, keepdims=True)
        y = d * lax.rsqrt(var + 1e-5) * lng_ref[...] + lnb_ref[...]
        ycv_ref[r0:r0 + rows, :] = (y * jax.nn.sigmoid(y)).astype(ycv_ref.dtype)
        acc2 = jnp.zeros((rows, c), F32)
        for j in range(sw):
            lo = r0 + CONV_HALO - (sw - 1) + j
            acc2 = acc2 + scw_ref[j:j + 1, :] * sbuf[lo:lo + rows, :]
        ysc_ref[r0:r0 + rows, :] = (gb_scr[r0:r0 + rows, :] * acc2).astype(ysc_ref.dtype)


def _conv_branches(h, g, w_in, conv_col0, dww, dwb, lng, lnb, scw, *, ts):
    b, s, d = h.shape
    c = dww.shape[1]
    ts = min(ts, s)
    assert dww.shape[0] - 1 <= CONV_HALO and scw.shape[0] - 1 <= CONV_HALO and conv_col0 % c == 0
    kern = functools.partial(_conv_kernel, ts=ts, rows=64)
    w_col = lambda n: pl.BlockSpec((d, c), lambda bi, t, n=n: (0, conv_col0 // c + n))
    full = lambda a: pl.BlockSpec(a.shape, lambda bi, t: (0, 0))
    vecs = [dww, dwb.reshape(1, c), lng.reshape(1, c), lnb.reshape(1, c), scw]
    out_spec = pl.BlockSpec((None, ts, c), lambda bi, t: (bi, t, 0))
    return pl.pallas_call(
        kern,
        grid=(b, s // ts),
        in_specs=[pl.BlockSpec((None, ts, d), lambda bi, t: (bi, t, 0)),
                  pl.BlockSpec((1, d), lambda bi, t: (0, 0))]
        + [w_col(n) for n in range(5)] + [full(a) for a in vecs],
        out_specs=[out_spec, out_spec],
        out_shape=[jax.ShapeDtypeStruct((b, s, c), BF16)] * 2,
        scratch_shapes=[pltpu.VMEM((CONV_HALO + ts, c), F32)] * 2 + [pltpu.VMEM((ts, c), F32)],
        compiler_params=_params("parallel", "arbitrary"),
        name="conv_branches",
    )(h, g.reshape(1, d), w_in, w_in, w_in, w_in, w_in, *vecs)


def _merge_kernel(ysb_ref, ycv_ref, ysc_ref, h_ref, g_ref, wg0_ref, wg1_ref, wg2_ref, bg_ref,
                  wb_ref, wo_ref, o_ref):
    h = h_ref[...]
    d = h.shape[1]
    xn = _rms(h, g_ref[...]).astype(BF16)
    merged = jnp.zeros(h.shape, F32)
    branches = ((ysb_ref, wg0_ref), (ycv_ref, wg1_ref), (ysc_ref, wg2_ref))
    for n, (y_ref, wg_ref) in enumerate(branches):
        proj = _dot(y_ref[...], wb_ref[n])
        gate = jax.nn.sigmoid(_dot(xn, wg_ref[...]) + bg_ref[:, n * d:(n + 1) * d])
        merged = merged + gate * proj
    o_ref[...] = h + _dot(merged.astype(BF16), wo_ref[...])


def _merge(ysb, ycv, ysc, h, g, w_in, gate_col0, b_gate, w_branch, w_out, *, tm):
    m, d = h.shape
    c = ysb.shape[1]
    nb = w_branch.shape[0]
    assert nb == 3 and gate_col0 % d == 0
    row = lambda w: pl.BlockSpec((tm, w), lambda i: (i, 0))
    gate_w = lambda n: pl.BlockSpec((d, d), lambda i, n=n: (0, gate_col0 // d + n))
    return pl.pallas_call(
        _merge_kernel,
        grid=(m // tm,),
        in_specs=[
            row(c), row(c), row(c), row(d),
            pl.BlockSpec((1, d), lambda i: (0, 0)),
            gate_w(0), gate_w(1), gate_w(2),
            pl.BlockSpec((1, nb * d), lambda i: (0, 0)),
            pl.BlockSpec((nb, c, d), lambda i: (0, 0, 0)),
            pl.BlockSpec((d, d), lambda i: (0, 0)),
        ],
        out_specs=row(d),
        out_shape=jax.ShapeDtypeStruct((m, d), F32),
        compiler_params=_params("parallel"),
        name="merge",
    )(ysb, ycv, ysc, h, g.reshape(1, d), w_in, w_in, w_in, b_gate.reshape(1, nb * d), w_branch, w_out)


def _xattn_kernel(h_ref, g_ref, wq_ref, kv_ref, wo_ref, o_ref):
    x = h_ref[...]
    d = x.shape[1]
    hd = d // XA_HEADS
    q = _dot(_rms(x, g_ref[...]).astype(BF16), wq_ref[...]).astype(BF16)
    outs = []
    for n in range(XA_HEADS):
        kh = kv_ref[:, n * hd:(n + 1) * hd]
        vh = kv_ref[:, d + n * hd:d + (n + 1) * hd]
        s = lax.dot_general(q[:, n * hd:(n + 1) * hd], kh, _NT, preferred_element_type=F32)
        s = s * (hd ** -0.5)
        e = jnp.exp(s - jnp.max(s, axis=-1, keepdims=True))
        a = e / jnp.sum(e, axis=-1, keepdims=True)
        outs.append(_dot(a.astype(BF16), vh).astype(BF16))
    o_ref[...] = x + _dot(jnp.concatenate(outs, axis=1), wo_ref[...])


def _cross_attention(h, g, wq, kv, wo, *, tm):
    b, s, d = h.shape
    n_mem = kv.shape[1]
    tm = min(tm, s)
    return pl.pallas_call(
        _xattn_kernel,
        grid=(b, s // tm),
        in_specs=[
            pl.BlockSpec((None, tm, d), lambda bi, i: (bi, i, 0)),
            pl.BlockSpec((1, d), lambda bi, i: (0, 0)),
            pl.BlockSpec((d, d), lambda bi, i: (0, 0)),
            pl.BlockSpec((None, n_mem, 2 * d), lambda bi, i: (bi, 0, 0)),
            pl.BlockSpec((d, d), lambda bi, i: (0, 0)),
        ],
        out_specs=pl.BlockSpec((None, tm, d), lambda bi, i: (bi, i, 0)),
        out_shape=jax.ShapeDtypeStruct((b, s, d), F32),
        compiler_params=_params("parallel", "parallel"),
        name="cross_attention",
    )(h, g.reshape(1, d), wq, kv, wo)


_KEY_POS = float(N_KEYS * N_KEYS)
_KEY_BIG = 2.0 ** 30

_SORT16 = (
    (0, 13), (1, 12), (2, 15), (3, 14), (4, 8), (5, 6), (7, 11), (9, 10),
    (0, 5), (1, 7), (2, 9), (3, 4), (6, 13), (8, 14), (10, 15), (11, 12),
    (0, 1), (2, 3), (4, 5), (6, 8), (7, 9), (10, 11), (12, 13), (14, 15),
    (0, 2), (1, 3), (4, 10), (5, 11), (6, 7), (8, 9), (12, 14), (13, 15),
    (1, 2), (3, 12), (4, 6), (5, 7), (8, 10), (9, 11), (13, 14),
    (1, 4), (2, 6), (5, 8), (7, 10), (9, 13), (11, 14),
    (2, 4), (3, 6), (9, 12), (11, 13),
    (3, 5), (6, 8), (7, 9), (10, 12),
    (3, 4), (5, 6), (7, 8), (9, 10), (11, 12),
    (6, 7), (8, 9),
)


def _sorted_topk(scores, k):
    depth = scores.shape[0] // SUBLANES
    assert depth == len({i for ce in _SORT16 for i in ce}) and k <= depth
    sub = lax.broadcasted_iota(jnp.int32, (SUBLANES, scores.shape[1]), 0).astype(F32)
    a = [scores[v * SUBLANES:(v + 1) * SUBLANES] for v in range(depth)]
    ix = [sub + float(v * SUBLANES) for v in range(depth)]
    for i, j in _SORT16:
        swap = (a[j] > a[i]) | ((a[j] == a[i]) & (ix[j] < ix[i]))
        a[i], a[j] = jnp.where(swap, a[j], a[i]), jnp.where(swap, a[i], a[j])
        ix[i], ix[j] = jnp.where(swap, ix[j], ix[i]), jnp.where(swap, ix[i], ix[j])
    vals, rows = [], []
    for n in range(k):
        m = jnp.max(a[0], axis=0, keepdims=True)
        row = jnp.min(jnp.where(a[0] == m, ix[0], float(scores.shape[0])), axis=0, keepdims=True)
        vals.append(m)
        rows.append(row)
        won = ix[0] == row
        for d in range(k - 1 - n):
            a[d] = jnp.where(won, a[d + 1], a[d])
            ix[d] = jnp.where(won, ix[d + 1], ix[d])
    return vals, rows


def _route_kernel(h_ref, g_ref, wq_ref, keys_ref, hn_ref, e1_ref, e2_ref, gate_ref,
                  q_scr, v_scr, i_scr, top_scr, key_scr, e1t_scr, e2t_scr, gt_scr, *, tm):
    k = PEER_TOPK
    hn = _rms(h_ref[...], g_ref[...]).astype(BF16)
    hn_ref[...] = hn
    q_scr[...] = _dot(hn, wq_ref[...]).astype(BF16)

    def head(h, carry):
        for p in range(2):
            hp = h * 2 + p
            qs = q_scr[:, pl.ds(pl.multiple_of(hp * N_KEYS, N_KEYS), N_KEYS)]
            s = lax.dot_general(keys_ref[hp], qs, _NT, preferred_element_type=F32)
            for t0 in range(0, tm, LANES):
                vals, rows = _sorted_topk(s[:, t0:t0 + LANES], k)
                for n in range(k):
                    v_scr[p, n:n + 1, t0:t0 + LANES] = vals[n]
                    i_scr[p, n:n + 1, t0:t0 + LANES] = rows[n]
        v1, i1, v2, i2 = v_scr[0], i_scr[0], v_scr[1], i_scr[1]
        cands, keys = [], []

        def add_block(i_lo, i_n, j_n):
            jrow = lax.broadcasted_iota(jnp.int32, (j_n, tm), 0).astype(F32)
            for i in range(i_lo, i_lo + i_n):
                cands.append(v1[i:i + 1] + v2[0:j_n])
                pos = float(i * k) + jrow
                keys.append(pos * _KEY_POS + i1[i:i + 1] * float(N_KEYS) + i2[0:j_n])

        add_block(0, 2, k)
        add_block(2, 6, k // 2)
        cands.append(v1[k // 2:k] + v2[0:1])
        irow = lax.broadcasted_iota(jnp.int32, (k // 2, tm), 0).astype(F32) + float(k // 2)
        keys.append(irow * float(k) * _KEY_POS + i1[k // 2:k] * float(N_KEYS) + i2[0:1])
        cand = jnp.concatenate(cands, axis=0)
        key = jnp.concatenate(keys, axis=0)

        def pick(n, cand):
            m = jnp.max(cand, axis=0, keepdims=True)
            kmin = jnp.min(jnp.where(cand == m, key, _KEY_BIG), axis=0, keepdims=True)
            top_scr[pl.ds(n, 1), :] = m
            key_scr[pl.ds(n, 1), :] = kmin
            return jnp.where(key == kmin, -jnp.inf, cand)

        lax.fori_loop(0, k, pick, cand)
        top = top_scr[...]
        kk = key_scr[...]
        e = kk - jnp.floor(kk * (1.0 / _KEY_POS)) * _KEY_POS
        e1 = jnp.floor(e * (1.0 / N_KEYS))
        e2 = e - e1 * float(N_KEYS)
        ex = jnp.exp(top - top[0:1])
        gate = ex / jnp.sum(ex, axis=0, keepdims=True)
        r0 = pl.multiple_of(h * k, k)
        e1t_scr[pl.ds(r0, k), :] = e1
        e2t_scr[pl.ds(r0, k), :] = e2
        gt_scr[pl.ds(r0, k), :] = gate
        return carry

    lax.fori_loop(0, PEER_HEADS, head, 0)
    e1_ref[...] = e1t_scr[...].T.astype(jnp.int32)
    e2_ref[...] = e2t_scr[...].T.astype(jnp.int32)
    gate_ref[...] = gt_scr[...].T


def _peer_route(h, g, wq, keys, *, tm):
    m, d = h.shape
    qw = wq.shape[1]
    nsel = PEER_HEADS * PEER_TOPK
    k = PEER_TOPK
    kern = functools.partial(_route_kernel, tm=tm)
    row = lambda w: pl.BlockSpec((tm, w), lambda i: (i, 0))
    return pl.pallas_call(
        kern,
        grid=(m // tm,),
        in_specs=[
            row(d),
            pl.BlockSpec((1, d), lambda i: (0, 0)),
            pl.BlockSpec((d, qw), lambda i: (0, 0)),
            pl.BlockSpec(keys.shape, lambda i: (0, 0, 0)),
        ],
        out_specs=[row(d), row(nsel), row(nsel), row(nsel)],
        out_shape=[
            jax.ShapeDtypeStruct((m, d), BF16),
            jax.ShapeDtypeStruct((m, nsel), jnp.int32),
            jax.ShapeDtypeStruct((m, nsel), jnp.int32),
            jax.ShapeDtypeStruct((m, nsel), F32),
        ],
        scratch_shapes=[
            pltpu.VMEM((tm, qw), BF16),
            pltpu.VMEM((2, k, tm), F32),
            pltpu.VMEM((2, k, tm), F32),
            pltpu.VMEM((k, tm), F32),
            pltpu.VMEM((k, tm), F32),
            pltpu.VMEM((nsel, tm), F32),
            pltpu.VMEM((nsel, tm), F32),
            pltpu.VMEM((nsel, tm), F32),
        ],
        compiler_params=_params("parallel"),
        name="peer_route",
    )(h, g.reshape(1, d), wq, keys)


def _asel_kernel(hn_ref, ut_ref, e1_ref, e2_ref, o_ref, a0_scr, a1_scr, *, chunks):
    grp = pl.program_id(1)

    @pl.when(grp == 0)
    def _():
        o_ref[...] = jnp.zeros(o_ref.shape, F32)

    hn = hn_ref[...]
    e1 = e1_ref[...]
    e2 = e2_ref[...]
    per_dot = DOT_COLS // N_KEYS
    nblk = chunks // per_dot
    assert nblk % 2 == 0

    def matmul(n, dst):
        cols = pl.ds(pl.multiple_of(n * DOT_COLS, DOT_COLS), DOT_COLS)
        dst[...] = _dot(hn, ut_ref[:, cols])

    def gather(n, src, sel):
        for c in range(per_dot):
            gathered = jnp.take_along_axis(src[:, c * N_KEYS:(c + 1) * N_KEYS], e2, axis=1)
            sel = jnp.where(e1 == grp * chunks + n * per_dot + c, gathered, sel)
        return sel

    def body(m, sel):
        matmul(2 * m + 1, a1_scr)
        sel = gather(2 * m, a0_scr, sel)
        matmul(2 * m + 2, a0_scr)
        return gather(2 * m + 1, a1_scr, sel)

    matmul(0, a0_scr)
    sel = lax.fori_loop(0, nblk // 2 - 1, body, o_ref[...])
    matmul(nblk - 1, a1_scr)
    sel = gather(nblk - 2, a0_scr, sel)
    o_ref[...] = gather(nblk - 1, a1_scr, sel)


def _peer_asel(hn, ut, e1, e2, *, tm, te):
    m, d = hn.shape
    ne = ut.shape[1]
    nsel = e1.shape[1]
    kern = functools.partial(_asel_kernel, chunks=te // N_KEYS)
    return pl.pallas_call(
        kern,
        grid=(m // tm, ne // te),
        in_specs=[
            pl.BlockSpec((tm, d), lambda i, j: (i, 0)),
            pl.BlockSpec((d, te), lambda i, j: (0, j)),
            pl.BlockSpec((tm, nsel), lambda i, j: (i, 0)),
            pl.BlockSpec((tm, nsel), lambda i, j: (i, 0)),
        ],
        out_specs=pl.BlockSpec((tm, nsel), lambda i, j: (i, 0)),
        out_shape=jax.ShapeDtypeStruct((m, nsel), F32),
        scratch_shapes=[pltpu.VMEM((tm, DOT_COLS), F32)] * 2,
        compiler_params=_params("parallel", "arbitrary"),
        name="peer_asel",
    )(hn, ut, e1, e2)


def _pout_kernel(h_ref, asel_ref, gate_ref, e1_ref, e2_ref, v_ref, o_ref, w_scr, w3_scr, g0_scr,
                 g1_scr, *, tm, chunks):
    grp = pl.program_id(1)

    @pl.when(grp == 0)
    def _():
        a = asel_ref[...]
        gelu = 0.5 * a * (1.0 + lax.erf(a * (2.0 ** -0.5)))
        w_scr[...] = gate_ref[...] * gelu
        o_ref[...] = h_ref[...]
        sub = lax.broadcasted_iota(jnp.int32, (N_KEYS, e1_ref.shape[1]), 0)

        def tiles(n, dst):
            t0 = pl.multiple_of(n * W_GROUP, W_GROUP)
            e1g = e1_ref[pl.ds(t0, W_GROUP), :]
            e2g = e2_ref[pl.ds(t0, W_GROUP), :]
            wg = w_scr[pl.ds(t0, W_GROUP), :]
            for t in range(W_GROUP):
                lhs = jnp.where(sub == e1g[t:t + 1], wg[t:t + 1], 0.0).astype(BF16)
                rhs = jnp.where(sub == e2g[t:t + 1], 1.0, 0.0).astype(BF16)
                dst[t * N_KEYS:(t + 1) * N_KEYS, :] = lax.dot_general(
                    lhs, rhs, _NT, preferred_element_type=F32)

        def regroup(n, src):
            t0 = pl.multiple_of(n * W_GROUP, W_GROUP)
            g = src[...].astype(BF16).reshape(W_GROUP, N_KEYS, N_KEYS)
            w3_scr[:, pl.ds(t0, W_GROUP), :] = jnp.swapaxes(g, 0, 1)

        groups = tm // W_GROUP
        assert groups % 2 == 0

        def body(m, carry):
            tiles(2 * m + 1, g1_scr)
            regroup(2 * m, g0_scr)
            tiles(2 * m + 2, g0_scr)
            regroup(2 * m + 1, g1_scr)
            return carry

        tiles(0, g0_scr)
        lax.fori_loop(0, groups // 2 - 1, body, 0)
        tiles(groups - 1, g1_scr)
        regroup(groups - 2, g0_scr)
        regroup(groups - 1, g1_scr)

    parts = [w3_scr[grp * chunks + c] for c in range(chunks)]
    o_ref[...] += _dot(jnp.concatenate(parts, axis=1), v_ref[...])


def _peer_out(h, asel, gate, e1, e2, v, *, tm, te):
    m, d = h.shape
    ne = v.shape[0]
    nsel = e1.shape[1]
    kern = functools.partial(_pout_kernel, tm=tm, chunks=te // N_KEYS)
    sel = pl.BlockSpec((tm, nsel), lambda i, j: (i, 0))
    return pl.pallas_call(
        kern,
        grid=(m // tm, ne // te),
        in_specs=[
            pl.BlockSpec((tm, d), lambda i, j: (i, 0)),
            sel, sel, sel, sel,
            pl.BlockSpec((te, d), lambda i, j: (j, 0)),
        ],
        out_specs=pl.BlockSpec((tm, d), lambda i, j: (i, 0)),
        out_shape=jax.ShapeDtypeStruct((m, d), F32),
        scratch_shapes=[pltpu.VMEM((tm, nsel), F32), pltpu.VMEM((N_KEYS, tm, N_KEYS), BF16),
                        pltpu.VMEM((W_GROUP * N_KEYS, N_KEYS), F32),
                        pltpu.VMEM((W_GROUP * N_KEYS, N_KEYS), F32)],
        compiler_params=_params("parallel", "arbitrary"),
        name="peer_out",
    )(h, asel, gate, e1, e2, v)


def _final_norm_kernel(x_ref, g_ref, o_ref):
    o_ref[...] = _rms(x_ref[...], g_ref[...])


def _final_norm(h, g, *, tm):
    m, d = h.shape
    return pl.pallas_call(
        _final_norm_kernel,
        grid=(m // tm,),
        in_specs=[pl.BlockSpec((tm, d), lambda i: (i, 0)), pl.BlockSpec((1, d), lambda i: (0, 0))],
        out_specs=pl.BlockSpec((tm, d), lambda i: (i, 0)),
        out_shape=jax.ShapeDtypeStruct((m, d), F32),
        compiler_params=_params("parallel"),
        name="final_norm",
    )(h, g.reshape(1, d))


def kernel(x, mem, norm_mix_g, w_in, conv_dw_w, conv_dw_b, conv_ln_g, conv_ln_b, sc_w, w_branch, b_gate, w_out, norm_xa_g, norm_mem_g, xa_wq, xa_wkv, xa_wo, norm_ffn_g, peer_wq, peer_keys, peer_u, peer_v, final_g):
    b, s, d = x.shape
    depth = w_in.shape[0]
    n_mem = mem.shape[1]
    sb_w = w_branch.shape[2]
    cv_w = conv_dw_w.shape[2]
    nb = w_branch.shape[1]
    qkv_w, conv_w = 3 * sb_w, 5 * cv_w
    m = b * s
    tm_big = min(1024, m)

    h = x.reshape(m, d)
    mem2 = mem.reshape(b * n_mem, d)
    for l in range(depth):
        w_in_l = w_in[l].astype(BF16)
        p_qkv = _norm_matmul(h, norm_mix_g[l], w_in_l, col0=0, ncols=qkv_w, out_dtype=BF16,
                             tm=tm_big, tn=512, name="in_proj_qkv")
        y_sb = _sb_attention(p_qkv.reshape(b, s, qkv_w), width=sb_w, bq=512, bk=256)
        y_cv, y_sc = _conv_branches(h.reshape(b, s, d), norm_mix_g[l], w_in_l, qkv_w, conv_dw_w[l],
                                    conv_dw_b[l], conv_ln_g[l], conv_ln_b[l], sc_w[l], ts=256)
        h = _merge(y_sb.reshape(m, sb_w), y_cv.reshape(m, cv_w), y_sc.reshape(m, cv_w), h,
                   norm_mix_g[l], w_in_l, qkv_w + conv_w, b_gate[l], w_branch[l].astype(BF16),
                   w_out[l].astype(BF16), tm=256)

        kv = _norm_matmul(mem2, norm_mem_g[l], xa_wkv[l].astype(BF16), col0=0, ncols=2 * d,
                          out_dtype=BF16, tm=min(1024, b * n_mem), tn=512, name="mem_kv")
        h = _cross_attention(h.reshape(b, s, d), norm_xa_g[l], xa_wq[l].astype(BF16),
                             kv.reshape(b, n_mem, 2 * d), xa_wo[l].astype(BF16), tm=256).reshape(m, d)

        keys = peer_keys[l].astype(BF16).reshape(PEER_HEADS * 2, N_KEYS, peer_keys.shape[-1])
        hn, e1, e2, gate = _peer_route(h, norm_ffn_g[l], peer_wq[l].astype(BF16), keys, tm=256)
        asel = _peer_asel(hn, peer_u[l].astype(BF16).T, e1, e2, tm=min(512, m), te=8192)
        h = _peer_out(h, asel, gate, e1, e2, peer_v[l].astype(BF16), tm=min(512, m), te=1024)
    return _final_norm(h, final_g, tm=tm_big).reshape(b, s, d)
```

```python
import functools

import jax
import jax.numpy as jnp
from jax import lax
from jax.experimental import pallas as pl
from jax.experimental.pallas import tpu as pltpu

F32 = jnp.float32
BF16 = jnp.bfloat16

LANES = 128
SUBLANES = 8
MXU_WIDTH = 256
DOT_COLS = 2 * MXU_WIDTH
VMEM_LIMIT_BYTES = 56 * 1024 * 1024

SB_HEAD_DIM = 64
SB_HEADS_PER_TILE = LANES // SB_HEAD_DIM
XA_HEADS = 4
PEER_HEADS = 8
PEER_TOPK = 16
N_KEYS = 128
CONV_HALO = 32
W_GROUP = 16

_NT = (((1,), (1,)), ((), ()))


def _params(*sem):
    return pltpu.CompilerParams(dimension_semantics=sem, vmem_limit_bytes=VMEM_LIMIT_BYTES)


def _rms(x, g, eps=1e-6):
    return x * lax.rsqrt(jnp.mean(x * x, axis=-1, keepdims=True) + eps) * g


def _dot(a, b):
    return jnp.dot(a, b, preferred_element_type=F32)


def _norm_matmul_kernel(x_ref, g_ref, w_ref, o_ref, xn_ref):
    @pl.when(pl.program_id(1) == 0)
    def _():
        xn_ref[...] = _rms(x_ref[...], g_ref[...]).astype(BF16)

    o_ref[...] = _dot(xn_ref[...], w_ref[...]).astype(o_ref.dtype)


def _norm_matmul(x, g, w, *, col0, ncols, out_dtype, tm, tn, name):
    m, k = x.shape
    off = col0 // tn
    assert col0 % tn == 0 and ncols % tn == 0 and m % tm == 0
    return pl.pallas_call(
        _norm_matmul_kernel,
        grid=(m // tm, ncols // tn),
        in_specs=[
            pl.BlockSpec((tm, k), lambda i, j: (i, 0)),
            pl.BlockSpec((1, k), lambda i, j: (0, 0)),
            pl.BlockSpec((k, tn), lambda i, j: (0, j + off)),
        ],
        out_specs=pl.BlockSpec((tm, tn), lambda i, j: (i, j)),
        out_shape=jax.ShapeDtypeStruct((m, ncols), out_dtype),
        scratch_shapes=[pltpu.VMEM((tm, k), BF16)],
        compiler_params=_params("parallel", "arbitrary"),
        name=name,
    )(x, g.reshape(1, k), w)


def _softplus(z):
    return jnp.maximum(z, 0.0) + jnp.log(1.0 + jnp.exp(-jnp.abs(z)))


def _sb_kernel(q_ref, k_ref, v_ref, o_ref, *, bq, bk, scale):
    i = pl.program_id(2)
    per_q = bq // bk
    q = q_ref[...]
    lane = lax.broadcasted_iota(jnp.int32, (bq, LANES), 1)
    trow = lax.broadcasted_iota(jnp.int32, (bk, bk), 0)
    tcol = lax.broadcasted_iota(jnp.int32, (bk, bk), 1)
    tri = jnp.where(trow > tcol, 1.0, 0.0).astype(BF16)
    tri2 = jnp.concatenate([tri, tri], axis=0)
    zero_q = jnp.zeros_like(q)
    heads = range(SB_HEADS_PER_TILE)
    qs = q * jnp.asarray(scale, q.dtype)
    qhs = [jnp.where((lane // SB_HEAD_DIM) == hh, qs, zero_q) for hh in heads]

    def step(carry, blocks):
        chains = [(h, n) for h in heads for n in range(len(blocks))]
        ks = [k_ref[pl.ds(pl.multiple_of(j * bk, bk), bk), :] for j, _, _ in blocks]
        vs = [v_ref[pl.ds(pl.multiple_of(j * bk, bk), bk), :] for j, _, _ in blocks]
        z = {(h, n): lax.dot_general(qhs[h][blocks[n][1]:], ks[n], _NT, preferred_element_type=F32)
             for h, n in chains}
        keep, lb, lk, split = {}, {}, {}, {}
        for ch in chains:
            sp = _softplus(z[ch])
            if blocks[ch[1]][2]:
                keep[ch] = (lax.broadcasted_iota(jnp.int32, sp.shape, 1)
                            < lax.broadcasted_iota(jnp.int32, sp.shape, 0))
                lk[ch] = jnp.where(keep[ch], -sp, 0.0)
            else:
                lk[ch] = -sp
            lb[ch] = z[ch] - sp
            hi = lk[ch].astype(BF16)
            lo = (lk[ch] - hi.astype(F32)).astype(BF16)
            split[ch] = jnp.concatenate([hi, lo], axis=1)
        after = {ch: _dot(split[ch], tri2) for ch in chains}
        mass = {ch: jnp.sum(lk[ch], axis=1, keepdims=True) for ch in chains}
        out = []
        for h in heads:
            acc, c = carry[h]
            for n, (_, r0, causal) in enumerate(blocks):
                a = jnp.exp(lb[h, n] + after[h, n] + c[r0:])
                if causal:
                    a = jnp.where(keep[h, n], a, 0.0)
                da, dc = _dot(a.astype(BF16), vs[n]), mass[h, n]
                if r0:
                    da = jnp.concatenate([jnp.zeros((r0, LANES), F32), da], axis=0)
                    dc = jnp.concatenate([jnp.zeros((r0, 1), F32), dc], axis=0)
                acc, c = acc + da, c + dc
            out.append((acc, c))
        return tuple(out)

    zero = (jnp.zeros((bq, LANES), F32), jnp.zeros((bq, 1), F32))
    carry = step(tuple(zero for _ in heads),
                 [(i * per_q + d, d * bk, True) for d in reversed(range(per_q))])
    n_left = i * per_q
    carry = lax.fori_loop(
        0, n_left // 2,
        lambda n, carry: step(
            carry, [(n_left - 1 - 2 * n, 0, False), (n_left - 2 - 2 * n, 0, False)]), carry)
    if per_q % 2:
        carry = lax.cond(n_left % 2 == 1, lambda carry: step(carry, [(0, 0, False)]),
                         lambda carry: carry, carry)
    o = carry[0][0]
    for hh in heads[1:]:
        o = jnp.where((lane // SB_HEAD_DIM) == hh, carry[hh][0], o)
    o_ref[...] = o.astype(o_ref.dtype)


def _sb_attention(p_qkv, *, width, bq, bk):
    b, s, _ = p_qkv.shape
    tiles = width // LANES
    bq, bk = min(bq, s), min(bk, s)
    assert bq % bk == 0 and s % bq == 0
    kern = functools.partial(_sb_kernel, bq=bq, bk=bk, scale=SB_HEAD_DIM ** -0.5)
    return pl.pallas_call(
        kern,
        grid=(b, tiles, s // bq),
        in_specs=[
            pl.BlockSpec((None, bq, LANES), lambda bi, t, i: (bi, i, t)),
            pl.BlockSpec((None, s, LANES), lambda bi, t, i: (bi, 0, tiles + t)),
            pl.BlockSpec((None, s, LANES), lambda bi, t, i: (bi, 0, 2 * tiles + t)),
        ],
        out_specs=pl.BlockSpec((None, bq, LANES), lambda bi, t, i: (bi, i, t)),
        out_shape=jax.ShapeDtypeStruct((b, s, width), BF16),
        compiler_params=_params("parallel", "parallel", "arbitrary"),
        name="sb_attention",
    )(p_qkv, p_qkv, p_qkv)


def _conv_kernel(h_ref, g_ref, w_val, w_gate, w_gb, w_gc, w_xin, dww_ref, dwb_ref, lng_ref, lnb_ref,
                 scw_ref, ycv_ref, ysc_ref, ubuf, sbuf, gb_scr, *, ts, rows):
    t = pl.program_id(1)
    c = ubuf.shape[1]
    xn = _rms(h_ref[...], g_ref[...]).astype(BF16)

    @pl.when(t == 0)
    def _():
        ubuf[0:CONV_HALO, :] = jnp.zeros((CONV_HALO, c), F32)
        sbuf[0:CONV_HALO, :] = jnp.zeros((CONV_HALO, c), F32)

    @pl.when(t > 0)
    def _():
        ubuf[0:CONV_HALO, :] = ubuf[ts:ts + CONV_HALO, :]
        sbuf[0:CONV_HALO, :] = sbuf[ts:ts + CONV_HALO, :]

    ubuf[CONV_HALO:, :] = _dot(xn, w_val[...]) * jax.nn.sigmoid(_dot(xn, w_gate[...]))
    sbuf[CONV_HALO:, :] = _dot(xn, w_gc[...]) * _dot(xn, w_xin[...])
    gb_scr[...] = _dot(xn, w_gb[...])
    kw = dww_ref.shape[0]
    sw = scw_ref.shape[0]

    for r0 in range(0, ts, rows):
        acc = jnp.zeros((rows, c), F32)
        for j in range(kw):
            lo = r0 + CONV_HALO - (kw - 1) + j
            acc = acc + dww_ref[j:j + 1, :] * ubuf[lo:lo + rows, :]
        acc = acc + dwb_ref[...]
        mu = jnp.mean(acc, axis=-1, keepdims=True)
        d = acc - mu
        var = jnp.mean(d * d, axis=-1, keepdims=True)
        y = d * lax.rsqrt(var + 1e-5) * lng_ref[...] + lnb_ref[...]
        ycv_ref[r0:r0 + rows, :] = (y * jax.nn.sigmoid(y)).astype(ycv_ref.dtype)
        acc2 = jnp.zeros((rows, c), F32)
        for j in range(sw):
            lo = r0 + CONV_HALO - (sw - 1) + j
            acc2 = acc2 + scw_ref[j:j + 1, :] * sbuf[lo:lo + rows, :]
        ysc_ref[r0:r0 + rows, :] = (gb_scr[r0:r0 + rows, :] * acc2).astype(ysc_ref.dtype)


def _conv_branches(h, g, w_in, conv_col0, dww, dwb, lng, lnb, scw, *, ts):
    b, s, d = h.shape
    c = dww.shape[1]
    ts = min(ts, s)
    assert dww.shape[0] - 1 <= CONV_HALO and scw.shape[0] - 1 <= CONV_HALO and conv_col0 % c == 0
    kern = functools.partial(_conv_kernel, ts=ts, rows=64)
    w_col = lambda n: pl.BlockSpec((d, c), lambda bi, t, n=n: (0, conv_col0 // c + n))
    full = lambda a: pl.BlockSpec(a.shape, lambda bi, t: (0, 0))
    vecs = [dww, dwb.reshape(1, c), lng.reshape(1, c), lnb.reshape(1, c), scw]
    out_spec = pl.BlockSpec((None, ts, c), lambda bi, t: (bi, t, 0))
    return pl.pallas_call(
        kern,
        grid=(b, s // ts),
        in_specs=[pl.BlockSpec((None, ts, d), lambda bi, t: (bi, t, 0)),
                  pl.BlockSpec((1, d), lambda bi, t: (0, 0))]
        + [w_col(n) for n in range(5)] + [full(a) for a in vecs],
        out_specs=[out_spec, out_spec],
        out_shape=[jax.ShapeDtypeStruct((b, s, c), BF16)] * 2,
        scratch_shapes=[pltpu.VMEM((CONV_HALO + ts, c), F32)] * 2 + [pltpu.VMEM((ts, c), F32)],
        compiler_params=_params("parallel", "arbitrary"),
        name="conv_branches",
    )(h, g.reshape(1, d), w_in, w_in, w_in, w_in, w_in, *vecs)


def _merge_kernel(ysb_ref, ycv_ref, ysc_ref, h_ref, g_ref, wg0_ref, wg1_ref, wg2_ref, bg_ref,
                  wb_ref, wo_ref, o_ref):
    h = h_ref[...]
    d = h.shape[1]
    xn = _rms(h, g_ref[...]).astype(BF16)
    merged = jnp.zeros(h.shape, F32)
    branches = ((ysb_ref, wg0_ref), (ycv_ref, wg1_ref), (ysc_ref, wg2_ref))
    for n, (y_ref, wg_ref) in enumerate(branches):
        proj = _dot(y_ref[...], wb_ref[n])
        gate = jax.nn.sigmoid(_dot(xn, wg_ref[...]) + bg_ref[:, n * d:(n + 1) * d])
        merged = merged + gate * proj
    o_ref[...] = h + _dot(merged.astype(BF16), wo_ref[...])


def _merge(ysb, ycv, ysc, h, g, w_in, gate_col0, b_gate, w_branch, w_out, *, tm):
    m, d = h.shape
    c = ysb.shape[1]
    nb = w_branch.shape[0]
    assert nb == 3 and gate_col0 % d == 0
    row = lambda w: pl.BlockSpec((tm, w), lambda i: (i, 0))
    gate_w = lambda n: pl.BlockSpec((d, d), lambda i, n=n: (0, gate_col0 // d + n))
    return pl.pallas_call(
        _merge_kernel,
        grid=(m // tm,),
        in_specs=[
            row(c), row(c), row(c), row(d),
            pl.BlockSpec((1, d), lambda i: (0, 0)),
            gate_w(0), gate_w(1), gate_w(2),
            pl.BlockSpec((1, nb * d), lambda i: (0, 0)),
            pl.BlockSpec((nb, c, d), lambda i: (0, 0, 0)),
            pl.BlockSpec((d, d), lambda i: (0, 0)),
        ],
        out_specs=row(d),
        out_shape=jax.ShapeDtypeStruct((m, d), F32),
        compiler_params=_params("parallel"),
        name="merge",
    )(ysb, ycv, ysc, h, g.reshape(1, d), w_in, w_in, w_in, b_gate.reshape(1, nb * d), w_branch, w_out)


def _xattn_kernel(h_ref, g_ref, wq_ref, kv_ref, wo_ref, o_ref):
    x = h_ref[...]
    d = x.shape[1]
    hd = d // XA_HEADS
    q = _dot(_rms(x, g_ref[...]).astype(BF16), wq_ref[...]).astype(BF16)
    outs = []
    for n in range(XA_HEADS):
        kh = kv_ref[:, n * hd:(n + 1) * hd]
        vh = kv_ref[:, d + n * hd:d + (n + 1) * hd]
        s = lax.dot_general(q[:, n * hd:(n + 1) * hd], kh, _NT, preferred_element_type=F32)
        s = s * (hd ** -0.5)
        e = jnp.exp(s - jnp.max(s, axis=-1, keepdims=True))
        a = e / jnp.sum(e, axis=-1, keepdims=True)
        outs.append(_dot(a.astype(BF16), vh).astype(BF16))
    o_ref[...] = x + _dot(jnp.concatenate(outs, axis=1), wo_ref[...])


def _cross_attention(h, g, wq, kv, wo, *, tm):
    b, s, d = h.shape
    n_mem = kv.shape[1]
    tm = min(tm, s)
    return pl.pallas_call(
        _xattn_kernel,
        grid=(b, s // tm),
        in_specs=[
            pl.BlockSpec((None, tm, d), lambda bi, i: (bi, i, 0)),
            pl.BlockSpec((1, d), lambda bi, i: (0, 0)),
            pl.BlockSpec((d, d), lambda bi, i: (0, 0)),
            pl.BlockSpec((None, n_mem, 2 * d), lambda bi, i: (bi, 0, 0)),
            pl.BlockSpec((d, d), lambda bi, i: (0, 0)),
        ],
        out_specs=pl.BlockSpec((None, tm, d), lambda bi, i: (bi, i, 0)),
        out_shape=jax.ShapeDtypeStruct((b, s, d), F32),
        compiler_params=_params("parallel", "parallel"),
        name="cross_attention",
    )(h, g.reshape(1, d), wq, kv, wo)


_KEY_POS = float(N_KEYS * N_KEYS)
_KEY_BIG = 2.0 ** 30

_SORT16 = (
    (0, 13), (1, 12), (2, 15), (3, 14), (4, 8), (5, 6), (7, 11), (9, 10),
    (0, 5), (1, 7), (2, 9), (3, 4), (6, 13), (8, 14), (10, 15), (11, 12),
    (0, 1), (2, 3), (4, 5), (6, 8), (7, 9), (10, 11), (12, 13), (14, 15),
    (0, 2), (1, 3), (4, 10), (5, 11), (6, 7), (8, 9), (12, 14), (13, 15),
    (1, 2), (3, 12), (4, 6), (5, 7), (8, 10), (9, 11), (13, 14),
    (1, 4), (2, 6), (5, 8), (7, 10), (9, 13), (11, 14),
    (2, 4), (3, 6), (9, 12), (11, 13),
    (3, 5), (6, 8), (7, 9), (10, 12),
    (3, 4), (5, 6), (7, 8), (9, 10), (11, 12),
    (6, 7), (8, 9),
)


def _sorted_topk(scores, k):
    depth = scores.shape[0] // SUBLANES
    assert depth == len({i for ce in _SORT16 for i in ce}) and k <= depth
    sub = lax.broadcasted_iota(jnp.int32, (SUBLANES, scores.shape[1]), 0).astype(F32)
    a = [scores[v * SUBLANES:(v + 1) * SUBLANES] for v in range(depth)]
    ix = [sub + float(v * SUBLANES) for v in range(depth)]
    for i, j in _SORT16:
        swap = (a[j] > a[i]) | ((a[j] == a[i]) & (ix[j] < ix[i]))
        a[i], a[j] = jnp.where(swap, a[j], a[i]), jnp.where(swap, a[i], a[j])
        ix[i], ix[j] = jnp.where(swap, ix[j], ix[i]), jnp.where(swap, ix[i], ix[j])
    vals, rows = [], []
    for n in range(k):
        m = jnp.max(a[0], axis=0, keepdims=True)
        row = jnp.min(jnp.where(a[0] == m, ix[0], float(scores.shape[0])), axis=0, keepdims=True)
        vals.append(m)
        rows.append(row)
        won = ix[0] == row
        for d in range(k - 1 - n):
            a[d] = jnp.where(won, a[d + 1], a[d])
            ix[d] = jnp.where(won, ix[d + 1], ix[d])
    return vals, rows


def _merge_pair_sums(v1, i1, v2, i2, k):
    irow = lax.broadcasted_iota(jnp.int32, v1.shape, 0)
    sums, keys = [], []
    for j in range(k):
        live = irow < k // (j + 1)
        sums.append(jnp.where(live, v1 + v2[j:j + 1], -jnp.inf))
        pos = (irow * k + j).astype(F32)
        keys.append(pos * _KEY_POS + i1 * float(N_KEYS) + i2[j:j + 1])
    tops, top_keys = [], []
    for n in range(k):
        m = jnp.max(sums[0], axis=0, keepdims=True)
        key = jnp.min(jnp.where(sums[0] == m, keys[0], _KEY_BIG), axis=0, keepdims=True)
        tops.append(m)
        top_keys.append(key)
        won = keys[0] == key
        for d in range(k - 1 - n):
            sums[d] = jnp.where(won, sums[d + 1], sums[d])
            keys[d] = jnp.where(won, keys[d + 1], keys[d])
    return tops, top_keys


def _route_kernel(h_ref, g_ref, wq_ref, keys_ref, hn_ref, e1_ref, e2_ref, gate_ref,
                  q_scr, v_scr, i_scr, top_scr, key_scr, e1t_scr, e2t_scr, gt_scr, *, tm):
    k = PEER_TOPK
    hn = _rms(h_ref[...], g_ref[...]).astype(BF16)
    hn_ref[...] = hn
    q_scr[...] = _dot(hn, wq_ref[...]).astype(BF16)

    def head(h, carry):
        for p in range(2):
            hp = h * 2 + p
            qs = q_scr[:, pl.ds(pl.multiple_of(hp * N_KEYS, N_KEYS), N_KEYS)]
            s = lax.dot_general(keys_ref[hp], qs, _NT, preferred_element_type=F32)
            for t0 in range(0, tm, LANES):
                vals, rows = _sorted_topk(s[:, t0:t0 + LANES], k)
                for n in range(k):
                    v_scr[p, n:n + 1, t0:t0 + LANES] = vals[n]
                    i_scr[p, n:n + 1, t0:t0 + LANES] = rows[n]
        for t0 in range(0, tm, LANES):
            lanes = slice(t0, t0 + LANES)
            tops, keys = _merge_pair_sums(v_scr[0, :, lanes], i_scr[0, :, lanes],
                                          v_scr[1, :, lanes], i_scr[1, :, lanes], k)
            for n in range(k):
                top_scr[n:n + 1, lanes] = tops[n]
                key_scr[n:n + 1, lanes] = keys[n]
        top = top_scr[...]
        kk = key_scr[...]
        e = kk - jnp.floor(kk * (1.0 / _KEY_POS)) * _KEY_POS
        e1 = jnp.floor(e * (1.0 / N_KEYS))
        e2 = e - e1 * float(N_KEYS)
        ex = jnp.exp(top - top[0:1])
        gate = ex / jnp.sum(ex, axis=0, keepdims=True)
        r0 = pl.multiple_of(h * k, k)
        e1t_scr[pl.ds(r0, k), :] = e1
        e2t_scr[pl.ds(r0, k), :] = e2
        gt_scr[pl.ds(r0, k), :] = gate
        return carry

    lax.fori_loop(0, PEER_HEADS, head, 0)
    e1_ref[...] = e1t_scr[...].T.astype(jnp.int32)
    e2_ref[...] = e2t_scr[...].T.astype(jnp.int32)
    gate_ref[...] = gt_scr[...].T


def _peer_route(h, g, wq, keys, *, tm):
    m, d = h.shape
    qw = wq.shape[1]
    nsel = PEER_HEADS * PEER_TOPK
    k = PEER_TOPK
    kern = functools.partial(_route_kernel, tm=tm)
    row = lambda w: pl.BlockSpec((tm, w), lambda i: (i, 0))
    return pl.pallas_call(
        kern,
        grid=(m // tm,),
        in_specs=[
            row(d),
            pl.BlockSpec((1, d), lambda i: (0, 0)),
            pl.BlockSpec((d, qw), lambda i: (0, 0)),
            pl.BlockSpec(keys.shape, lambda i: (0, 0, 0)),
        ],
        out_specs=[row(d), row(nsel), row(nsel), row(nsel)],
        out_shape=[
            jax.ShapeDtypeStruct((m, d), BF16),
            jax.ShapeDtypeStruct((m, nsel), jnp.int32),
            jax.ShapeDtypeStruct((m, nsel), jnp.int32),
            jax.ShapeDtypeStruct((m, nsel), F32),
        ],
        scratch_shapes=[
            pltpu.VMEM((tm, qw), BF16),
            pltpu.VMEM((2, k, tm), F32),
            pltpu.VMEM((2, k, tm), F32),
            pltpu.VMEM((k, tm), F32),
            pltpu.VMEM((k, tm), F32),
            pltpu.VMEM((nsel, tm), F32),
            pltpu.VMEM((nsel, tm), F32),
            pltpu.VMEM((nsel, tm), F32),
        ],
        compiler_params=_params("parallel"),
        name="peer_route",
    )(h, g.reshape(1, d), wq, keys)


def _asel_kernel(hn_ref, ut_ref, e1_ref, e2_ref, o_ref, a0_scr, a1_scr, *, chunks):
    grp = pl.program_id(1)

    @pl.when(grp == 0)
    def _():
        o_ref[...] = jnp.zeros(o_ref.shape, F32)

    hn = hn_ref[...]
    e1 = e1_ref[...]
    e2 = e2_ref[...]
    per_dot = DOT_COLS // N_KEYS
    nblk = chunks // per_dot
    assert nblk % 2 == 0

    def matmul(n, dst):
        cols = pl.ds(pl.multiple_of(n * DOT_COLS, DOT_COLS), DOT_COLS)
        dst[...] = _dot(hn, ut_ref[:, cols])

    def gather(n, src, sel):
        for c in range(per_dot):
            gathered = jnp.take_along_axis(src[:, c * N_KEYS:(c + 1) * N_KEYS], e2, axis=1)
            sel = jnp.where(e1 == grp * chunks + n * per_dot + c, gathered, sel)
        return sel

    def body(m, sel):
        matmul(2 * m + 1, a1_scr)
        sel = gather(2 * m, a0_scr, sel)
        matmul(2 * m + 2, a0_scr)
        return gather(2 * m + 1, a1_scr, sel)

    matmul(0, a0_scr)
    sel = lax.fori_loop(0, nblk // 2 - 1, body, o_ref[...])
    matmul(nblk - 1, a1_scr)
    sel = gather(nblk - 2, a0_scr, sel)
    o_ref[...] = gather(nblk - 1, a1_scr, sel)


def _peer_asel(hn, ut, e1, e2, *, tm, te):
    m, d = hn.shape
    ne = ut.shape[1]
    nsel = e1.shape[1]
    kern = functools.partial(_asel_kernel, chunks=te // N_KEYS)
    return pl.pallas_call(
        kern,
        grid=(m // tm, ne // te),
        in_specs=[
            pl.BlockSpec((tm, d), lambda i, j: (i, 0)),
            pl.BlockSpec((d, te), lambda i, j: (0, j)),
            pl.BlockSpec((tm, nsel), lambda i, j: (i, 0)),
            pl.BlockSpec((tm, nsel), lambda i, j: (i, 0)),
        ],
        out_specs=pl.BlockSpec((tm, nsel), lambda i, j: (i, 0)),
        out_shape=jax.ShapeDtypeStruct((m, nsel), F32),
        scratch_shapes=[pltpu.VMEM((tm, DOT_COLS), F32)] * 2,
        compiler_params=_params("parallel", "arbitrary"),
        name="peer_asel",
    )(hn, ut, e1, e2)


def _pout_kernel(h_ref, asel_ref, gate_ref, e1_ref, e2_ref, v_ref, o_ref, w_scr, w3_scr, g0_scr,
                 g1_scr, *, tm, chunks):
    grp = pl.program_id(1)

    @pl.when(grp == 0)
    def _():
        a = asel_ref[...]
        gelu = 0.5 * a * (1.0 + lax.erf(a * (2.0 ** -0.5)))
        w_scr[...] = gate_ref[...] * gelu
        o_ref[...] = h_ref[...]
        sub = lax.broadcasted_iota(jnp.int32, (N_KEYS, e1_ref.shape[1]), 0)

        def tiles(n, dst):
            t0 = pl.multiple_of(n * W_GROUP, W_GROUP)
            e1g = e1_ref[pl.ds(t0, W_GROUP), :]
            e2g = e2_ref[pl.ds(t0, W_GROUP), :]
            wg = w_scr[pl.ds(t0, W_GROUP), :]
            for t in range(W_GROUP):
                lhs = jnp.where(sub == e1g[t:t + 1], wg[t:t + 1], 0.0).astype(BF16)
                rhs = jnp.where(sub == e2g[t:t + 1], 1.0, 0.0).astype(BF16)
                dst[t * N_KEYS:(t + 1) * N_KEYS, :] = lax.dot_general(
                    lhs, rhs, _NT, preferred_element_type=F32)

        def regroup(n, src):
            t0 = pl.multiple_of(n * W_GROUP, W_GROUP)
            g = src[...].astype(BF16).reshape(W_GROUP, N_KEYS, N_KEYS)
            w3_scr[:, pl.ds(t0, W_GROUP), :] = jnp.swapaxes(g, 0, 1)

        groups = tm // W_GROUP
        assert groups % 2 == 0

        def body(m, carry):
            tiles(2 * m + 1, g1_scr)
            regroup(2 * m, g0_scr)
            tiles(2 * m + 2, g0_scr)
            regroup(2 * m + 1, g1_scr)
            return carry

        tiles(0, g0_scr)
        lax.fori_loop(0, groups // 2 - 1, body, 0)
        tiles(groups - 1, g1_scr)
        regroup(groups - 2, g0_scr)
        regroup(groups - 1, g1_scr)

    parts = [w3_scr[grp * chunks + c] for c in range(chunks)]
    o_ref[...] += _dot(jnp.concatenate(parts, axis=1), v_ref[...])


def _peer_out(h, asel, gate, e1, e2, v, *, tm, te):
    m, d = h.shape
    ne = v.shape[0]
    nsel = e1.shape[1]
    kern = functools.partial(_pout_kernel, tm=tm, chunks=te // N_KEYS)
    sel = pl.BlockSpec((tm, nsel), lambda i, j: (i, 0))
    return pl.pallas_call(
        kern,
        grid=(m // tm, ne // te),
        in_specs=[
            pl.BlockSpec((tm, d), lambda i, j: (i, 0)),
            sel, sel, sel, sel,
            pl.BlockSpec((te, d), lambda i, j: (j, 0)),
        ],
        out_specs=pl.BlockSpec((tm, d), lambda i, j: (i, 0)),
        out_shape=jax.ShapeDtypeStruct((m, d), F32),
        scratch_shapes=[pltpu.VMEM((tm, nsel), F32), pltpu.VMEM((N_KEYS, tm, N_KEYS), BF16),
                        pltpu.VMEM((W_GROUP * N_KEYS, N_KEYS), F32),
                        pltpu.VMEM((W_GROUP * N_KEYS, N_KEYS), F32)],
        compiler_params=_params("parallel", "arbitrary"),
        name="peer_out",
    )(h, asel, gate, e1, e2, v)


def _final_norm_kernel(x_ref, g_ref, o_ref):
    o_ref[...] = _rms(x_ref[...], g_ref[...])


def _final_norm(h, g, *, tm):
    m, d = h.shape
    return pl.pallas_call(
        _final_norm_kernel,
        grid=(m // tm,),
        in_specs=[pl.BlockSpec((tm, d), lambda i: (i, 0)), pl.BlockSpec((1, d), lambda i: (0, 0))],
        out_specs=pl.BlockSpec((tm, d), lambda i: (i, 0)),
        out_shape=jax.ShapeDtypeStruct((m, d), F32),
        compiler_params=_params("parallel"),
        name="final_norm",
    )(h, g.reshape(1, d))


def kernel(x, mem, norm_mix_g, w_in, conv_dw_w, conv_dw_b, conv_ln_g, conv_ln_b, sc_w, w_branch, b_gate, w_out, norm_xa_g, norm_mem_g, xa_wq, xa_wkv, xa_wo, norm_ffn_g, peer_wq, peer_keys, peer_u, peer_v, final_g):
    b, s, d = x.shape
    depth = w_in.shape[0]
    n_mem = mem.shape[1]
    sb_w = w_branch.shape[2]
    cv_w = conv_dw_w.shape[2]
    nb = w_branch.shape[1]
    qkv_w, conv_w = 3 * sb_w, 5 * cv_w
    m = b * s
    tm_big = min(1024, m)

    h = x.reshape(m, d)
    mem2 = mem.reshape(b * n_mem, d)
    for l in range(depth):
        w_in_l = w_in[l].astype(BF16)
        p_qkv = _norm_matmul(h, norm_mix_g[l], w_in_l, col0=0, ncols=qkv_w, out_dtype=BF16,
                             tm=tm_big, tn=512, name="in_proj_qkv")
        y_sb = _sb_attention(p_qkv.reshape(b, s, qkv_w), width=sb_w, bq=512, bk=256)
        y_cv, y_sc = _conv_branches(h.reshape(b, s, d), norm_mix_g[l], w_in_l, qkv_w, conv_dw_w[l],
                                    conv_dw_b[l], conv_ln_g[l], conv_ln_b[l], sc_w[l], ts=256)
        h = _merge(y_sb.reshape(m, sb_w), y_cv.reshape(m, cv_w), y_sc.reshape(m, cv_w), h,
                   norm_mix_g[l], w_in_l, qkv_w + conv_w, b_gate[l], w_branch[l].astype(BF16),
                   w_out[l].astype(BF16), tm=256)

        kv = _norm_matmul(mem2, norm_mem_g[l], xa_wkv[l].astype(BF16), col0=0, ncols=2 * d,
                          out_dtype=BF16, tm=min(1024, b * n_mem), tn=512, name="mem_kv")
        h = _cross_attention(h.reshape(b, s, d), norm_xa_g[l], xa_wq[l].astype(BF16),
                             kv.reshape(b, n_mem, 2 * d), xa_wo[l].astype(BF16), tm=256).reshape(m, d)

        keys = peer_keys[l].astype(BF16).reshape(PEER_HEADS * 2, N_KEYS, peer_keys.shape[-1])
        hn, e1, e2, gate = _peer_route(h, norm_ffn_g[l], peer_wq[l].astype(BF16), keys, tm=256)
        asel = _peer_asel(hn, peer_u[l].astype(BF16).T, e1, e2, tm=min(512, m), te=8192)
        h = _peer_out(h, asel, gate, e1, e2, peer_v[l].astype(BF16), tm=min(512, m), te=1024)
    return _final_norm(h, final_g, tm=tm_big).reshape(b, s, d)
```

```python
import functools

import jax
import jax.numpy as jnp
from jax import lax
from jax.experimental import pallas as pl
from jax.experimental.pallas import tpu as pltpu

F32 = jnp.float32
BF16 = jnp.bfloat16

LANES = 128
SUBLANES = 8
MXU_WIDTH = 256
DOT_COLS = 2 * MXU_WIDTH
VMEM_LIMIT_BYTES = 56 * 1024 * 1024

SB_HEAD_DIM = 64
SB_HEADS_PER_TILE = LANES // SB_HEAD_DIM
XA_HEADS = 4
PEER_HEADS = 8
PEER_TOPK = 16
N_KEYS = 128
CONV_HALO = 32
W_GROUP = 16

_NT = (((1,), (1,)), ((), ()))


def _params(*sem):
    return pltpu.CompilerParams(dimension_semantics=sem, vmem_limit_bytes=VMEM_LIMIT_BYTES)


def _rms(x, g, eps=1e-6):
    return x * lax.rsqrt(jnp.mean(x * x, axis=-1, keepdims=True) + eps) * g


def _dot(a, b):
    return jnp.dot(a, b, preferred_element_type=F32)


def _norm_matmul_kernel(x_ref, g_ref, w_ref, o_ref, xn_ref):
    @pl.when(pl.program_id(1) == 0)
    def _():
        xn_ref[...] = _rms(x_ref[...], g_ref[...]).astype(BF16)

    o_ref[...] = _dot(xn_ref[...], w_ref[...]).astype(o_ref.dtype)


def _norm_matmul(x, g, w, *, col0, ncols, out_dtype, tm, tn, name):
    m, k = x.shape
    off = col0 // tn
    assert col0 % tn == 0 and ncols % tn == 0 and m % tm == 0
    return pl.pallas_call(
        _norm_matmul_kernel,
        grid=(m // tm, ncols // tn),
        in_specs=[
            pl.BlockSpec((tm, k), lambda i, j: (i, 0)),
            pl.BlockSpec((1, k), lambda i, j: (0, 0)),
            pl.BlockSpec((k, tn), lambda i, j: (0, j + off)),
        ],
        out_specs=pl.BlockSpec((tm, tn), lambda i, j: (i, j)),
        out_shape=jax.ShapeDtypeStruct((m, ncols), out_dtype),
        scratch_shapes=[pltpu.VMEM((tm, k), BF16)],
        compiler_params=_params("parallel", "arbitrary"),
        name=name,
    )(x, g.reshape(1, k), w)


def _softplus(z):
    return jnp.maximum(z, 0.0) + jnp.log(1.0 + jnp.exp(-jnp.abs(z)))


def _sb_kernel(q_ref, k_ref, v_ref, o_ref, *, bq, bk, scale):
    i = pl.program_id(2)
    per_q = bq // bk
    q = q_ref[...]
    lane = lax.broadcasted_iota(jnp.int32, (bq, LANES), 1)
    trow = lax.broadcasted_iota(jnp.int32, (bk, bk), 0)
    tcol = lax.broadcasted_iota(jnp.int32, (bk, bk), 1)
    tri = jnp.where(trow > tcol, 1.0, 0.0).astype(BF16)
    tri2 = jnp.concatenate([tri, tri], axis=0)
    zero_q = jnp.zeros_like(q)
    heads = range(SB_HEADS_PER_TILE)
    qs = q * jnp.asarray(scale, q.dtype)
    qhs = [jnp.where((lane // SB_HEAD_DIM) == hh, qs, zero_q) for hh in heads]

    def step(carry, blocks):
        chains = [(h, n) for h in heads for n in range(len(blocks))]
        ks = [k_ref[pl.ds(pl.multiple_of(j * bk, bk), bk), :] for j, _, _ in blocks]
        vs = [v_ref[pl.ds(pl.multiple_of(j * bk, bk), bk), :] for j, _, _ in blocks]
        z = {(h, n): lax.dot_general(qhs[h][blocks[n][1]:], ks[n], _NT, preferred_element_type=F32)
             for h, n in chains}
        keep, lb, lk, split = {}, {}, {}, {}
        for ch in chains:
            sp = _softplus(z[ch])
            if blocks[ch[1]][2]:
                keep[ch] = (lax.broadcasted_iota(jnp.int32, sp.shape, 1)
                            < lax.broadcasted_iota(jnp.int32, sp.shape, 0))
                lk[ch] = jnp.where(keep[ch], -sp, 0.0)
            else:
                lk[ch] = -sp
            lb[ch] = z[ch] - sp
            hi = lk[ch].astype(BF16)
            lo = (lk[ch] - hi.astype(F32)).astype(BF16)
            split[ch] = jnp.concatenate([hi, lo], axis=1)
        after = {ch: _dot(split[ch], tri2) for ch in chains}
        mass = {ch: jnp.sum(lk[ch], axis=1, keepdims=True) for ch in chains}
        out = []
        for h in heads:
            acc, c = carry[h]
            for n, (_, r0, causal) in enumerate(blocks):
                a = jnp.exp(lb[h, n] + after[h, n] + c[r0:])
                if causal:
                    a = jnp.where(keep[h, n], a, 0.0)
                da, dc = _dot(a.astype(BF16), vs[n]), mass[h, n]
                if r0:
                    da = jnp.concatenate([jnp.zeros((r0, LANES), F32), da], axis=0)
                    dc = jnp.concatenate([jnp.zeros((r0, 1), F32), dc], axis=0)
                acc, c = acc + da, c + dc
            out.append((acc, c))
        return tuple(out)

    zero = (jnp.zeros((bq, LANES), F32), jnp.zeros((bq, 1), F32))
    carry = step(tuple(zero for _ in heads),
                 [(i * per_q + d, d * bk, True) for d in reversed(range(per_q))])
    n_left = i * per_q
    carry = lax.fori_loop(
        0, n_left // 2,
        lambda n, carry: step(
            carry, [(n_left - 1 - 2 * n, 0, False), (n_left - 2 - 2 * n, 0, False)]), carry)
    if per_q % 2:
        carry = lax.cond(n_left % 2 == 1, lambda carry: step(carry, [(0, 0, False)]),
                         lambda carry: carry, carry)
    o = carry[0][0]
    for hh in heads[1:]:
        o = jnp.where((lane // SB_HEAD_DIM) == hh, carry[hh][0], o)
    o_ref[...] = o.astype(o_ref.dtype)


def _sb_attention(p_qkv, *, width, bq, bk):
    b, s, _ = p_qkv.shape
    tiles = width // LANES
    bq, bk = min(bq, s), min(bk, s)
    assert bq % bk == 0 and s % bq == 0
    kern = functools.partial(_sb_kernel, bq=bq, bk=bk, scale=SB_HEAD_DIM ** -0.5)
    return pl.pallas_call(
        kern,
        grid=(b, tiles, s // bq),
        in_specs=[
            pl.BlockSpec((None, bq, LANES), lambda bi, t, i: (bi, i, t)),
            pl.BlockSpec((None, s, LANES), lambda bi, t, i: (bi, 0, tiles + t)),
            pl.BlockSpec((None, s, LANES), lambda bi, t, i: (bi, 0, 2 * tiles + t)),
        ],
        out_specs=pl.BlockSpec((None, bq, LANES), lambda bi, t, i: (bi, i, t)),
        out_shape=jax.ShapeDtypeStruct((b, s, width), BF16),
        compiler_params=_params("parallel", "parallel", "arbitrary"),
        name="sb_attention",
    )(p_qkv, p_qkv, p_qkv)


def _conv_kernel(h_ref, g_ref, w_val, w_gate, w_gb, w_gc, w_xin, dww_ref, dwb_ref, lng_ref, lnb_ref,
                 scw_ref, ycv_ref, ysc_ref, ubuf, sbuf, gb_scr, win_scr, *, ts, rows):
    t = pl.program_id(1)
    c = ubuf.shape[1]
    xn = _rms(h_ref[...], g_ref[...]).astype(BF16)

    @pl.when(t == 0)
    def _():
        ubuf[0:CONV_HALO, :] = jnp.zeros((CONV_HALO, c), F32)
        sbuf[0:CONV_HALO, :] = jnp.zeros((CONV_HALO, c), F32)

    @pl.when(t > 0)
    def _():
        ubuf[0:CONV_HALO, :] = ubuf[ts:ts + CONV_HALO, :]
        sbuf[0:CONV_HALO, :] = sbuf[ts:ts + CONV_HALO, :]

    ubuf[CONV_HALO:, :] = _dot(xn, w_val[...]) * jax.nn.sigmoid(_dot(xn, w_gate[...]))
    sbuf[CONV_HALO:, :] = _dot(xn, w_gc[...]) * _dot(xn, w_xin[...])
    gb_scr[...] = _dot(xn, w_gb[...])
    kw = dww_ref.shape[0]
    sw = scw_ref.shape[0]

    first = CONV_HALO - (kw - 1)
    for r0 in range(0, ts, rows):
        acc = jnp.zeros((rows, c), F32)
        for phase in range(SUBLANES):
            taps = [j for j in range(kw) if (first + j) % SUBLANES == phase]
            if not taps:
                continue
            lo, hi = first + taps[0], first + taps[-1]
            span = hi - lo + rows
            win_scr[phase, 0:span, :] = ubuf[r0 + lo:r0 + lo + span, :]
            for j in taps:
                off = first + j - lo
                acc = acc + dww_ref[j:j + 1, :] * win_scr[phase, off:off + rows, :]
        acc = acc + dwb_ref[...]
        mu = jnp.mean(acc, axis=-1, keepdims=True)
        d = acc - mu
        var = jnp.mean(d * d, axis=-1, keepdims=True)
        y = d * lax.rsqrt(var + 1e-5) * lng_ref[...] + lnb_ref[...]
        ycv_ref[r0:r0 + rows, :] = (y * jax.nn.sigmoid(y)).astype(ycv_ref.dtype)
        acc2 = jnp.zeros((rows, c), F32)
        for j in range(sw):
            lo = r0 + CONV_HALO - (sw - 1) + j
            acc2 = acc2 + scw_ref[j:j + 1, :] * sbuf[lo:lo + rows, :]
        ysc_ref[r0:r0 + rows, :] = (gb_scr[r0:r0 + rows, :] * acc2).astype(ysc_ref.dtype)


def _conv_branches(h, g, w_in, conv_col0, dww, dwb, lng, lnb, scw, *, ts):
    b, s, d = h.shape
    c = dww.shape[1]
    ts = min(ts, s)
    assert dww.shape[0] - 1 <= CONV_HALO and scw.shape[0] - 1 <= CONV_HALO and conv_col0 % c == 0
    rows = 64
    kern = functools.partial(_conv_kernel, ts=ts, rows=rows)
    w_col = lambda n: pl.BlockSpec((d, c), lambda bi, t, n=n: (0, conv_col0 // c + n))
    full = lambda a: pl.BlockSpec(a.shape, lambda bi, t: (0, 0))
    vecs = [dww, dwb.reshape(1, c), lng.reshape(1, c), lnb.reshape(1, c), scw]
    out_spec = pl.BlockSpec((None, ts, c), lambda bi, t: (bi, t, 0))
    return pl.pallas_call(
        kern,
        grid=(b, s // ts),
        in_specs=[pl.BlockSpec((None, ts, d), lambda bi, t: (bi, t, 0)),
                  pl.BlockSpec((1, d), lambda bi, t: (0, 0))]
        + [w_col(n) for n in range(5)] + [full(a) for a in vecs],
        out_specs=[out_spec, out_spec],
        out_shape=[jax.ShapeDtypeStruct((b, s, c), BF16)] * 2,
        scratch_shapes=[pltpu.VMEM((CONV_HALO + ts, c), F32)] * 2 + [
            pltpu.VMEM((ts, c), F32), pltpu.VMEM((SUBLANES, CONV_HALO + rows, c), F32)],
        compiler_params=_params("parallel", "arbitrary"),
        name="conv_branches",
    )(h, g.reshape(1, d), w_in, w_in, w_in, w_in, w_in, *vecs)


def _merge_kernel(ysb_ref, ycv_ref, ysc_ref, h_ref, g_ref, wg0_ref, wg1_ref, wg2_ref, bg_ref,
                  wb_ref, wo_ref, o_ref):
    h = h_ref[...]
    d = h.shape[1]
    xn = _rms(h, g_ref[...]).astype(BF16)
    merged = jnp.zeros(h.shape, F32)
    branches = ((ysb_ref, wg0_ref), (ycv_ref, wg1_ref), (ysc_ref, wg2_ref))
    for n, (y_ref, wg_ref) in enumerate(branches):
        proj = _dot(y_ref[...], wb_ref[n])
        gate = jax.nn.sigmoid(_dot(xn, wg_ref[...]) + bg_ref[:, n * d:(n + 1) * d])
        merged = merged + gate * proj
    o_ref[...] = h + _dot(merged.astype(BF16), wo_ref[...])


def _merge(ysb, ycv, ysc, h, g, w_in, gate_col0, b_gate, w_branch, w_out, *, tm):
    m, d = h.shape
    c = ysb.shape[1]
    nb = w_branch.shape[0]
    assert nb == 3 and gate_col0 % d == 0
    row = lambda w: pl.BlockSpec((tm, w), lambda i: (i, 0))
    gate_w = lambda n: pl.BlockSpec((d, d), lambda i, n=n: (0, gate_col0 // d + n))
    return pl.pallas_call(
        _merge_kernel,
        grid=(m // tm,),
        in_specs=[
            row(c), row(c), row(c), row(d),
            pl.BlockSpec((1, d), lambda i: (0, 0)),
            gate_w(0), gate_w(1), gate_w(2),
            pl.BlockSpec((1, nb * d), lambda i: (0, 0)),
            pl.BlockSpec((nb, c, d), lambda i: (0, 0, 0)),
            pl.BlockSpec((d, d), lambda i: (0, 0)),
        ],
        out_specs=row(d),
        out_shape=jax.ShapeDtypeStruct((m, d), F32),
        compiler_params=_params("parallel"),
        name="merge",
    )(ysb, ycv, ysc, h, g.reshape(1, d), w_in, w_in, w_in, b_gate.reshape(1, nb * d), w_branch, w_out)


def _xattn_kernel(h_ref, g_ref, wq_ref, kv_ref, wo_ref, o_ref):
    x = h_ref[...]
    d = x.shape[1]
    hd = d // XA_HEADS
    q = _dot(_rms(x, g_ref[...]).astype(BF16), wq_ref[...]).astype(BF16)
    outs = []
    for n in range(XA_HEADS):
        kh = kv_ref[:, n * hd:(n + 1) * hd]
        vh = kv_ref[:, d + n * hd:d + (n + 1) * hd]
        s = lax.dot_general(q[:, n * hd:(n + 1) * hd], kh, _NT, preferred_element_type=F32)
        s = s * (hd ** -0.5)
        e = jnp.exp(s - jnp.max(s, axis=-1, keepdims=True))
        a = e / jnp.sum(e, axis=-1, keepdims=True)
        outs.append(_dot(a.astype(BF16), vh).astype(BF16))
    o_ref[...] = x + _dot(jnp.concatenate(outs, axis=1), wo_ref[...])


def _cross_attention(h, g, wq, kv, wo, *, tm):
    b, s, d = h.shape
    n_mem = kv.shape[1]
    tm = min(tm, s)
    return pl.pallas_call(
        _xattn_kernel,
        grid=(b, s // tm),
        in_specs=[
            pl.BlockSpec((None, tm, d), lambda bi, i: (bi, i, 0)),
            pl.BlockSpec((1, d), lambda bi, i: (0, 0)),
            pl.BlockSpec((d, d), lambda bi, i: (0, 0)),
            pl.BlockSpec((None, n_mem, 2 * d), lambda bi, i: (bi, 0, 0)),
            pl.BlockSpec((d, d), lambda bi, i: (0, 0)),
        ],
        out_specs=pl.BlockSpec((None, tm, d), lambda bi, i: (bi, i, 0)),
        out_shape=jax.ShapeDtypeStruct((b, s, d), F32),
        compiler_params=_params("parallel", "parallel"),
        name="cross_attention",
    )(h, g.reshape(1, d), wq, kv, wo)


_KEY_POS = float(N_KEYS * N_KEYS)
_KEY_BIG = 2.0 ** 30

_SORT16 = (
    (0, 13), (1, 12), (2, 15), (3, 14), (4, 8), (5, 6), (7, 11), (9, 10),
    (0, 5), (1, 7), (2, 9), (3, 4), (6, 13), (8, 14), (10, 15), (11, 12),
    (0, 1), (2, 3), (4, 5), (6, 8), (7, 9), (10, 11), (12, 13), (14, 15),
    (0, 2), (1, 3), (4, 10), (5, 11), (6, 7), (8, 9), (12, 14), (13, 15),
    (1, 2), (3, 12), (4, 6), (5, 7), (8, 10), (9, 11), (13, 14),
    (1, 4), (2, 6), (5, 8), (7, 10), (9, 13), (11, 14),
    (2, 4), (3, 6), (9, 12), (11, 13),
    (3, 5), (6, 8), (7, 9), (10, 12),
    (3, 4), (5, 6), (7, 8), (9, 10), (11, 12),
    (6, 7), (8, 9),
)


def _sorted_topk(scores, k):
    depth = scores.shape[0] // SUBLANES
    assert depth == len({i for ce in _SORT16 for i in ce}) and k <= depth
    sub = lax.broadcasted_iota(jnp.int32, (SUBLANES, scores.shape[1]), 0).astype(F32)
    a = [scores[v * SUBLANES:(v + 1) * SUBLANES] for v in range(depth)]
    ix = [sub + float(v * SUBLANES) for v in range(depth)]
    for i, j in _SORT16:
        swap = (a[j] > a[i]) | ((a[j] == a[i]) & (ix[j] < ix[i]))
        a[i], a[j] = jnp.where(swap, a[j], a[i]), jnp.where(swap, a[i], a[j])
        ix[i], ix[j] = jnp.where(swap, ix[j], ix[i]), jnp.where(swap, ix[i], ix[j])
    vals, rows = [], []
    for n in range(k):
        m = jnp.max(a[0], axis=0, keepdims=True)
        row = jnp.min(jnp.where(a[0] == m, ix[0], float(scores.shape[0])), axis=0, keepdims=True)
        vals.append(m)
        rows.append(row)
        won = ix[0] == row
        for d in range(k - 1 - n):
            a[d] = jnp.where(won, a[d + 1], a[d])
            ix[d] = jnp.where(won, ix[d + 1], ix[d])
    return vals, rows


def _merge_pair_sums(v1, i1, v2, i2, k):
    irow = lax.broadcasted_iota(jnp.int32, v1.shape, 0)
    sums, keys = [], []
    for j in range(k):
        live = irow < k // (j + 1)
        sums.append(jnp.where(live, v1 + v2[j:j + 1], -jnp.inf))
        pos = (irow * k + j).astype(F32)
        keys.append(pos * _KEY_POS + i1 * float(N_KEYS) + i2[j:j + 1])
    tops, top_keys = [], []
    for n in range(k):
        m = jnp.max(sums[0], axis=0, keepdims=True)
        key = jnp.min(jnp.where(sums[0] == m, keys[0], _KEY_BIG), axis=0, keepdims=True)
        tops.append(m)
        top_keys.append(key)
        won = keys[0] == key
        for d in range(k - 1 - n):
            sums[d] = jnp.where(won, sums[d + 1], sums[d])
            keys[d] = jnp.where(won, keys[d + 1], keys[d])
    return tops, top_keys


def _route_kernel(h_ref, g_ref, wq_ref, keys_ref, hn_ref, e1_ref, e2_ref, gate_ref,
                  q_scr, v_scr, i_scr, top_scr, key_scr, e1t_scr, e2t_scr, gt_scr, *, tm):
    k = PEER_TOPK
    hn = _rms(h_ref[...], g_ref[...]).astype(BF16)
    hn_ref[...] = hn
    q_scr[...] = _dot(hn, wq_ref[...]).astype(BF16)

    def head(h, carry):
        for p in range(2):
            hp = h * 2 + p
            qs = q_scr[:, pl.ds(pl.multiple_of(hp * N_KEYS, N_KEYS), N_KEYS)]
            s = lax.dot_general(keys_ref[hp], qs, _NT, preferred_element_type=F32)
            for t0 in range(0, tm, LANES):
                vals, rows = _sorted_topk(s[:, t0:t0 + LANES], k)
                for n in range(k):
                    v_scr[p, n:n + 1, t0:t0 + LANES] = vals[n]
                    i_scr[p, n:n + 1, t0:t0 + LANES] = rows[n]
        for t0 in range(0, tm, LANES):
            lanes = slice(t0, t0 + LANES)
            tops, keys = _merge_pair_sums(v_scr[0, :, lanes], i_scr[0, :, lanes],
                                          v_scr[1, :, lanes], i_scr[1, :, lanes], k)
            for n in range(k):
                top_scr[n:n + 1, lanes] = tops[n]
                key_scr[n:n + 1, lanes] = keys[n]
        top = top_scr[...]
        kk = key_scr[...]
        e = kk - jnp.floor(kk * (1.0 / _KEY_POS)) * _KEY_POS
        e1 = jnp.floor(e * (1.0 / N_KEYS))
        e2 = e - e1 * float(N_KEYS)
        ex = jnp.exp(top - top[0:1])
        gate = ex / jnp.sum(ex, axis=0, keepdims=True)
        r0 = pl.multiple_of(h * k, k)
        e1t_scr[pl.ds(r0, k), :] = e1
        e2t_scr[pl.ds(r0, k), :] = e2
        gt_scr[pl.ds(r0, k), :] = gate
        return carry

    lax.fori_loop(0, PEER_HEADS, head, 0)
    e1_ref[...] = e1t_scr[...].T.astype(jnp.int32)
    e2_ref[...] = e2t_scr[...].T.astype(jnp.int32)
    gate_ref[...] = gt_scr[...].T


def _peer_route(h, g, wq, keys, *, tm):
    m, d = h.shape
    qw = wq.shape[1]
    nsel = PEER_HEADS * PEER_TOPK
    k = PEER_TOPK
    kern = functools.partial(_route_kernel, tm=tm)
    row = lambda w: pl.BlockSpec((tm, w), lambda i: (i, 0))
    return pl.pallas_call(
        kern,
        grid=(m // tm,),
        in_specs=[
            row(d),
            pl.BlockSpec((1, d), lambda i: (0, 0)),
            pl.BlockSpec((d, qw), lambda i: (0, 0)),
            pl.BlockSpec(keys.shape, lambda i: (0, 0, 0)),
        ],
        out_specs=[row(d), row(nsel), row(nsel), row(nsel)],
        out_shape=[
            jax.ShapeDtypeStruct((m, d), BF16),
            jax.ShapeDtypeStruct((m, nsel), jnp.int32),
            jax.ShapeDtypeStruct((m, nsel), jnp.int32),
            jax.ShapeDtypeStruct((m, nsel), F32),
        ],
        scratch_shapes=[
            pltpu.VMEM((tm, qw), BF16),
            pltpu.VMEM((2, k, tm), F32),
            pltpu.VMEM((2, k, tm), F32),
            pltpu.VMEM((k, tm), F32),
            pltpu.VMEM((k, tm), F32),
            pltpu.VMEM((nsel, tm), F32),
            pltpu.VMEM((nsel, tm), F32),
            pltpu.VMEM((nsel, tm), F32),
        ],
        compiler_params=_params("parallel"),
        name="peer_route",
    )(h, g.reshape(1, d), wq, keys)


def _asel_kernel(hn_ref, ut_ref, e1_ref, e2_ref, o_ref, a0_scr, a1_scr, *, chunks):
    grp = pl.program_id(1)

    @pl.when(grp == 0)
    def _():
        o_ref[...] = jnp.zeros(o_ref.shape, F32)

    hn = hn_ref[...]
    e1 = e1_ref[...]
    e2 = e2_ref[...]
    per_dot = DOT_COLS // N_KEYS
    nblk = chunks // per_dot
    assert nblk % 2 == 0

    def matmul(n, dst):
        cols = pl.ds(pl.multiple_of(n * DOT_COLS, DOT_COLS), DOT_COLS)
        dst[...] = _dot(hn, ut_ref[:, cols])

    def gather(n, src, sel):
        for c in range(per_dot):
            gathered = jnp.take_along_axis(src[:, c * N_KEYS:(c + 1) * N_KEYS], e2, axis=1)
            sel = jnp.where(e1 == grp * chunks + n * per_dot + c, gathered, sel)
        return sel

    def body(m, sel):
        matmul(2 * m + 1, a1_scr)
        sel = gather(2 * m, a0_scr, sel)
        matmul(2 * m + 2, a0_scr)
        return gather(2 * m + 1, a1_scr, sel)

    matmul(0, a0_scr)
    sel = lax.fori_loop(0, nblk // 2 - 1, body, o_ref[...])
    matmul(nblk - 1, a1_scr)
    sel = gather(nblk - 2, a0_scr, sel)
    o_ref[...] = gather(nblk - 1, a1_scr, sel)


def _peer_asel(hn, ut, e1, e2, *, tm, te):
    m, d = hn.shape
    ne = ut.shape[1]
    nsel = e1.shape[1]
    kern = functools.partial(_asel_kernel, chunks=te // N_KEYS)
    return pl.pallas_call(
        kern,
        grid=(m // tm, ne // te),
        in_specs=[
            pl.BlockSpec((tm, d), lambda i, j: (i, 0)),
            pl.BlockSpec((d, te), lambda i, j: (0, j)),
            pl.BlockSpec((tm, nsel), lambda i, j: (i, 0)),
            pl.BlockSpec((tm, nsel), lambda i, j: (i, 0)),
        ],
        out_specs=pl.BlockSpec((tm, nsel), lambda i, j: (i, 0)),
        out_shape=jax.ShapeDtypeStruct((m, nsel), F32),
        scratch_shapes=[pltpu.VMEM((tm, DOT_COLS), F32)] * 2,
        compiler_params=_params("parallel", "arbitrary"),
        name="peer_asel",
    )(hn, ut, e1, e2)


def _pout_kernel(h_ref, asel_ref, gate_ref, e1_ref, e2_ref, v_ref, o_ref, w_scr, w3_scr, g0_scr,
                 g1_scr, *, tm, chunks):
    grp = pl.program_id(1)

    @pl.when(grp == 0)
    def _():
        a = asel_ref[...]
        gelu = 0.5 * a * (1.0 + lax.erf(a * (2.0 ** -0.5)))
        w_scr[...] = gate_ref[...] * gelu
        o_ref[...] = h_ref[...]
        sub = lax.broadcasted_iota(jnp.int32, (N_KEYS, e1_ref.shape[1]), 0)

        def tiles(n, dst):
            t0 = pl.multiple_of(n * W_GROUP, W_GROUP)
            e1g = e1_ref[pl.ds(t0, W_GROUP), :]
            e2g = e2_ref[pl.ds(t0, W_GROUP), :]
            wg = w_scr[pl.ds(t0, W_GROUP), :]
            for t in range(W_GROUP):
                lhs = jnp.where(sub == e1g[t:t + 1], wg[t:t + 1], 0.0).astype(BF16)
                rhs = jnp.where(sub == e2g[t:t + 1], 1.0, 0.0).astype(BF16)
                dst[t * N_KEYS:(t + 1) * N_KEYS, :] = lax.dot_general(
                    lhs, rhs, _NT, preferred_element_type=F32)

        def regroup(n, src):
            t0 = pl.multiple_of(n * W_GROUP, W_GROUP)
            g = src[...].astype(BF16).reshape(W_GROUP, N_KEYS, N_KEYS)
            w3_scr[:, pl.ds(t0, W_GROUP), :] = jnp.swapaxes(g, 0, 1)

        groups = tm // W_GROUP
        assert groups % 2 == 0

        def body(m, carry):
            tiles(2 * m + 1, g1_scr)
            regroup(2 * m, g0_scr)
            tiles(2 * m + 2, g0_scr)
            regroup(2 * m + 1, g1_scr)
            return carry

        tiles(0, g0_scr)
        lax.fori_loop(0, groups // 2 - 1, body, 0)
        tiles(groups - 1, g1_scr)
        regroup(groups - 2, g0_scr)
        regroup(groups - 1, g1_scr)

    parts = [w3_scr[grp * chunks + c] for c in range(chunks)]
    o_ref[...] += _dot(jnp.concatenate(parts, axis=1), v_ref[...])


def _peer_out(h, asel, gate, e1, e2, v, *, tm, te):
    m, d = h.shape
    ne = v.shape[0]
    nsel = e1.shape[1]
    kern = functools.partial(_pout_kernel, tm=tm, chunks=te // N_KEYS)
    sel = pl.BlockSpec((tm, nsel), lambda i, j: (i, 0))
    return pl.pallas_call(
        kern,
        grid=(m // tm, ne // te),
        in_specs=[
            pl.BlockSpec((tm, d), lambda i, j: (i, 0)),
            sel, sel, sel, sel,
            pl.BlockSpec((te, d), lambda i, j: (j, 0)),
        ],
        out_specs=pl.BlockSpec((tm, d), lambda i, j: (i, 0)),
        out_shape=jax.ShapeDtypeStruct((m, d), F32),
        scratch_shapes=[pltpu.VMEM((tm, nsel), F32), pltpu.VMEM((N_KEYS, tm, N_KEYS), BF16),
                        pltpu.VMEM((W_GROUP * N_KEYS, N_KEYS), F32),
                        pltpu.VMEM((W_GROUP * N_KEYS, N_KEYS), F32)],
        compiler_params=_params("parallel", "arbitrary"),
        name="peer_out",
    )(h, asel, gate, e1, e2, v)


def _final_norm_kernel(x_ref, g_ref, o_ref):
    o_ref[...] = _rms(x_ref[...], g_ref[...])


def _final_norm(h, g, *, tm):
    m, d = h.shape
    return pl.pallas_call(
        _final_norm_kernel,
        grid=(m // tm,),
        in_specs=[pl.BlockSpec((tm, d), lambda i: (i, 0)), pl.BlockSpec((1, d), lambda i: (0, 0))],
        out_specs=pl.BlockSpec((tm, d), lambda i: (i, 0)),
        out_shape=jax.ShapeDtypeStruct((m, d), F32),
        compiler_params=_params("parallel"),
        name="final_norm",
    )(h, g.reshape(1, d))


def kernel(x, mem, norm_mix_g, w_in, conv_dw_w, conv_dw_b, conv_ln_g, conv_ln_b, sc_w, w_branch, b_gate, w_out, norm_xa_g, norm_mem_g, xa_wq, xa_wkv, xa_wo, norm_ffn_g, peer_wq, peer_keys, peer_u, peer_v, final_g):
    b, s, d = x.shape
    depth = w_in.shape[0]
    n_mem = mem.shape[1]
    sb_w = w_branch.shape[2]
    cv_w = conv_dw_w.shape[2]
    nb = w_branch.shape[1]
    qkv_w, conv_w = 3 * sb_w, 5 * cv_w
    m = b * s
    tm_big = min(1024, m)

    h = x.reshape(m, d)
    mem2 = mem.reshape(b * n_mem, d)
    for l in range(depth):
        w_in_l = w_in[l].astype(BF16)
        p_qkv = _norm_matmul(h, norm_mix_g[l], w_in_l, col0=0, ncols=qkv_w, out_dtype=BF16,
                             tm=tm_big, tn=512, name="in_proj_qkv")
        y_sb = _sb_attention(p_qkv.reshape(b, s, qkv_w), width=sb_w, bq=512, bk=256)
        y_cv, y_sc = _conv_branches(h.reshape(b, s, d), norm_mix_g[l], w_in_l, qkv_w, conv_dw_w[l],
                                    conv_dw_b[l], conv_ln_g[l], conv_ln_b[l], sc_w[l], ts=256)
        h = _merge(y_sb.reshape(m, sb_w), y_cv.reshape(m, cv_w), y_sc.reshape(m, cv_w), h,
                   norm_mix_g[l], w_in_l, qkv_w + conv_w, b_gate[l], w_branch[l].astype(BF16),
                   w_out[l].astype(BF16), tm=256)

        kv = _norm_matmul(mem2, norm_mem_g[l], xa_wkv[l].astype(BF16), col0=0, ncols=2 * d,
                          out_dtype=BF16, tm=min(1024, b * n_mem), tn=512, name="mem_kv")
        h = _cross_attention(h.reshape(b, s, d), norm_xa_g[l], xa_wq[l].astype(BF16),
                             kv.reshape(b, n_mem, 2 * d), xa_wo[l].astype(BF16), tm=256).reshape(m, d)

        keys = peer_keys[l].astype(BF16).reshape(PEER_HEADS * 2, N_KEYS, peer_keys.shape[-1])
        hn, e1, e2, gate = _peer_route(h, norm_ffn_g[l], peer_wq[l].astype(BF16), keys, tm=256)
        asel = _peer_asel(hn, peer_u[l].T.astype(BF16), e1, e2, tm=min(512, m), te=8192)
        h = _peer_out(h, asel, gate, e1, e2, peer_v[l].astype(BF16), tm=min(512, m), te=4096)
    return _final_norm(h, final_g, tm=tm_big).reshape(b, s, d)
```

```python
import functools

import jax
import jax.numpy as jnp
from jax import lax
from jax.experimental import pallas as pl
from jax.experimental.pallas import tpu as pltpu

F32 = jnp.float32
BF16 = jnp.bfloat16

LANES = 128
SUBLANES = 8
MXU_WIDTH = 256
DOT_COLS = 2 * MXU_WIDTH
VMEM_LIMIT_BYTES = 56 * 1024 * 1024

SB_HEAD_DIM = 64
SB_HEADS_PER_TILE = LANES // SB_HEAD_DIM
XA_HEADS = 4
PEER_HEADS = 8
PEER_TOPK = 16
N_KEYS = 128
CONV_HALO = 32
ROUTE_HEADS_PER_TRIP = 4
W_GROUP = 16

_NT = (((1,), (1,)), ((), ()))


def _params(*sem):
    return pltpu.CompilerParams(dimension_semantics=sem, vmem_limit_bytes=VMEM_LIMIT_BYTES)


def _rms(x, g, eps=1e-6):
    return x * lax.rsqrt(jnp.mean(x * x, axis=-1, keepdims=True) + eps) * g


def _dot(a, b):
    return jnp.dot(a, b, preferred_element_type=F32)


def _norm_matmul_kernel(x_ref, g_ref, w_ref, o_ref, xn_ref):
    @pl.when(pl.program_id(1) == 0)
    def _():
        xn_ref[...] = _rms(x_ref[...], g_ref[...]).astype(BF16)

    o_ref[...] = _dot(xn_ref[...], w_ref[...]).astype(o_ref.dtype)


def _norm_matmul(x, g, w, *, col0, ncols, out_dtype, tm, tn, name):
    m, k = x.shape
    off = col0 // tn
    assert col0 % tn == 0 and ncols % tn == 0 and m % tm == 0
    return pl.pallas_call(
        _norm_matmul_kernel,
        grid=(m // tm, ncols // tn),
        in_specs=[
            pl.BlockSpec((tm, k), lambda i, j: (i, 0)),
            pl.BlockSpec((1, k), lambda i, j: (0, 0)),
            pl.BlockSpec((k, tn), lambda i, j: (0, j + off)),
        ],
        out_specs=pl.BlockSpec((tm, tn), lambda i, j: (i, j)),
        out_shape=jax.ShapeDtypeStruct((m, ncols), out_dtype),
        scratch_shapes=[pltpu.VMEM((tm, k), BF16)],
        compiler_params=_params("parallel", "arbitrary"),
        name=name,
    )(x, g.reshape(1, k), w)


def _softplus(z):
    return jnp.maximum(z, 0.0) + jnp.log(1.0 + jnp.exp(-jnp.abs(z)))


def _sb_kernel(q_ref, k_ref, v_ref, o_ref, *, bq, bk, scale):
    i = pl.program_id(2)
    per_q = bq // bk
    q = q_ref[...]
    lane = lax.broadcasted_iota(jnp.int32, (bq, LANES), 1)
    trow = lax.broadcasted_iota(jnp.int32, (bk, bk), 0)
    tcol = lax.broadcasted_iota(jnp.int32, (bk, bk), 1)
    tri = jnp.where(trow > tcol, 1.0, 0.0).astype(BF16)
    tri2 = jnp.concatenate([tri, tri], axis=0)
    zero_q = jnp.zeros_like(q)
    heads = range(SB_HEADS_PER_TILE)
    qs = q * jnp.asarray(scale, q.dtype)
    qhs = [jnp.where((lane // SB_HEAD_DIM) == hh, qs, zero_q) for hh in heads]

    def step(carry, blocks):
        chains = [(h, n) for h in heads for n in range(len(blocks))]
        ks = [k_ref[pl.ds(pl.multiple_of(j * bk, bk), bk), :] for j, _, _ in blocks]
        vs = [v_ref[pl.ds(pl.multiple_of(j * bk, bk), bk), :] for j, _, _ in blocks]
        z = {(h, n): lax.dot_general(qhs[h][blocks[n][1]:], ks[n], _NT, preferred_element_type=F32)
             for h, n in chains}
        keep, lb, lk, split = {}, {}, {}, {}
        for ch in chains:
            sp = _softplus(z[ch])
            if blocks[ch[1]][2]:
                keep[ch] = (lax.broadcasted_iota(jnp.int32, sp.shape, 1)
                            < lax.broadcasted_iota(jnp.int32, sp.shape, 0))
                lk[ch] = jnp.where(keep[ch], -sp, 0.0)
            else:
                lk[ch] = -sp
            lb[ch] = z[ch] - sp
            hi = lk[ch].astype(BF16)
            lo = (lk[ch] - hi.astype(F32)).astype(BF16)
            split[ch] = jnp.concatenate([hi, lo], axis=1)
        after = {ch: _dot(split[ch], tri2) for ch in chains}
        mass = {ch: jnp.sum(lk[ch], axis=1, keepdims=True) for ch in chains}
        out = []
        for h in heads:
            acc, c = carry[h]
            for n, (_, r0, causal) in enumerate(blocks):
                a = jnp.exp(lb[h, n] + after[h, n] + c[r0:])
                if causal:
                    a = jnp.where(keep[h, n], a, 0.0)
                da, dc = _dot(a.astype(BF16), vs[n]), mass[h, n]
                if r0:
                    da = jnp.concatenate([jnp.zeros((r0, LANES), F32), da], axis=0)
                    dc = jnp.concatenate([jnp.zeros((r0, 1), F32), dc], axis=0)
                acc, c = acc + da, c + dc
            out.append((acc, c))
        return tuple(out)

    zero = (jnp.zeros((bq, LANES), F32), jnp.zeros((bq, 1), F32))
    carry = step(tuple(zero for _ in heads),
                 [(i * per_q + d, d * bk, True) for d in reversed(range(per_q))])
    n_left = i * per_q
    carry = lax.fori_loop(
        0, n_left // 2,
        lambda n, carry: step(
            carry, [(n_left - 1 - 2 * n, 0, False), (n_left - 2 - 2 * n, 0, False)]), carry)
    if per_q % 2:
        carry = lax.cond(n_left % 2 == 1, lambda carry: step(carry, [(0, 0, False)]),
                         lambda carry: carry, carry)
    o = carry[0][0]
    for hh in heads[1:]:
        o = jnp.where((lane // SB_HEAD_DIM) == hh, carry[hh][0], o)
    o_ref[...] = o.astype(o_ref.dtype)


def _sb_attention(p_qkv, *, width, bq, bk):
    b, s, _ = p_qkv.shape
    tiles = width // LANES
    bq, bk = min(bq, s), min(bk, s)
    assert bq % bk == 0 and s % bq == 0
    kern = functools.partial(_sb_kernel, bq=bq, bk=bk, scale=SB_HEAD_DIM ** -0.5)
    return pl.pallas_call(
        kern,
        grid=(b, tiles, s // bq),
        in_specs=[
            pl.BlockSpec((None, bq, LANES), lambda bi, t, i: (bi, i, t)),
            pl.BlockSpec((None, s, LANES), lambda bi, t, i: (bi, 0, tiles + t)),
            pl.BlockSpec((None, s, LANES), lambda bi, t, i: (bi, 0, 2 * tiles + t)),
        ],
        out_specs=pl.BlockSpec((None, bq, LANES), lambda bi, t, i: (bi, i, t)),
        out_shape=jax.ShapeDtypeStruct((b, s, width), BF16),
        compiler_params=_params("parallel", "parallel", "arbitrary"),
        name="sb_attention",
    )(p_qkv, p_qkv, p_qkv)


def _conv_kernel(h_ref, g_ref, w_val, w_gate, w_gb, w_gc, w_xin, dww_ref, dwb_ref, lng_ref, lnb_ref,
                 scw_ref, ycv_ref, ysc_ref, ubuf, sbuf, gb_scr, win_scr, *, ts, rows):
    t = pl.program_id(1)
    c = ubuf.shape[1]
    xn = _rms(h_ref[...], g_ref[...]).astype(BF16)

    @pl.when(t == 0)
    def _():
        ubuf[0:CONV_HALO, :] = jnp.zeros((CONV_HALO, c), F32)
        sbuf[0:CONV_HALO, :] = jnp.zeros((CONV_HALO, c), F32)

    @pl.when(t > 0)
    def _():
        ubuf[0:CONV_HALO, :] = ubuf[ts:ts + CONV_HALO, :]
        sbuf[0:CONV_HALO, :] = sbuf[ts:ts + CONV_HALO, :]

    ubuf[CONV_HALO:, :] = _dot(xn, w_val[...]) * jax.nn.sigmoid(_dot(xn, w_gate[...]))
    sbuf[CONV_HALO:, :] = _dot(xn, w_gc[...]) * _dot(xn, w_xin[...])
    gb_scr[...] = _dot(xn, w_gb[...])
    kw = dww_ref.shape[0]
    sw = scw_ref.shape[0]

    first = CONV_HALO - (kw - 1)
    for r0 in range(0, ts, rows):
        acc = jnp.zeros((rows, c), F32)
        for phase in range(SUBLANES):
            taps = [j for j in range(kw) if (first + j) % SUBLANES == phase]
            if not taps:
                continue
            lo, hi = first + taps[0], first + taps[-1]
            span = hi - lo + rows
            win_scr[phase, 0:span, :] = ubuf[r0 + lo:r0 + lo + span, :]
            for j in taps:
                off = first + j - lo
                acc = acc + dww_ref[j:j + 1, :] * win_scr[phase, off:off + rows, :]
        acc = acc + dwb_ref[...]
        mu = jnp.mean(acc, axis=-1, keepdims=True)
        d = acc - mu
        var = jnp.mean(d * d, axis=-1, keepdims=True)
        y = d * lax.rsqrt(var + 1e-5) * lng_ref[...] + lnb_ref[...]
        ycv_ref[r0:r0 + rows, :] = (y * jax.nn.sigmoid(y)).astype(ycv_ref.dtype)
        acc2 = jnp.zeros((rows, c), F32)
        for j in range(sw):
            lo = r0 + CONV_HALO - (sw - 1) + j
            acc2 = acc2 + scw_ref[j:j + 1, :] * sbuf[lo:lo + rows, :]
        ysc_ref[r0:r0 + rows, :] = (gb_scr[r0:r0 + rows, :] * acc2).astype(ysc_ref.dtype)


def _conv_branches(h, g, w_in, conv_col0, dww, dwb, lng, lnb, scw, *, ts):
    b, s, d = h.shape
    c = dww.shape[1]
    ts = min(ts, s)
    assert dww.shape[0] - 1 <= CONV_HALO and scw.shape[0] - 1 <= CONV_HALO and conv_col0 % c == 0
    rows = 64
    kern = functools.partial(_conv_kernel, ts=ts, rows=rows)
    w_col = lambda n: pl.BlockSpec((d, c), lambda bi, t, n=n: (0, conv_col0 // c + n))
    full = lambda a: pl.BlockSpec(a.shape, lambda bi, t: (0, 0))
    vecs = [dww, dwb.reshape(1, c), lng.reshape(1, c), lnb.reshape(1, c), scw]
    out_spec = pl.BlockSpec((None, ts, c), lambda bi, t: (bi, t, 0))
    return pl.pallas_call(
        kern,
        grid=(b, s // ts),
        in_specs=[pl.BlockSpec((None, ts, d), lambda bi, t: (bi, t, 0)),
                  pl.BlockSpec((1, d), lambda bi, t: (0, 0))]
        + [w_col(n) for n in range(5)] + [full(a) for a in vecs],
        out_specs=[out_spec, out_spec],
        out_shape=[jax.ShapeDtypeStruct((b, s, c), BF16)] * 2,
        scratch_shapes=[pltpu.VMEM((CONV_HALO + ts, c), F32)] * 2 + [
            pltpu.VMEM((ts, c), F32), pltpu.VMEM((SUBLANES, CONV_HALO + rows, c), F32)],
        compiler_params=_params("parallel", "arbitrary"),
        name="conv_branches",
    )(h, g.reshape(1, d), w_in, w_in, w_in, w_in, w_in, *vecs)


def _merge_kernel(ysb_ref, ycv_ref, ysc_ref, h_ref, g_ref, wg0_ref, wg1_ref, wg2_ref, bg_ref,
                  wb_ref, wo_ref, o_ref):
    h = h_ref[...]
    d = h.shape[1]
    xn = _rms(h, g_ref[...]).astype(BF16)
    merged = jnp.zeros(h.shape, F32)
    branches = ((ysb_ref, wg0_ref), (ycv_ref, wg1_ref), (ysc_ref, wg2_ref))
    for n, (y_ref, wg_ref) in enumerate(branches):
        proj = _dot(y_ref[...], wb_ref[n])
        gate = jax.nn.sigmoid(_dot(xn, wg_ref[...]) + bg_ref[:, n * d:(n + 1) * d])
        merged = merged + gate * proj
    o_ref[...] = h + _dot(merged.astype(BF16), wo_ref[...])


def _merge(ysb, ycv, ysc, h, g, w_in, gate_col0, b_gate, w_branch, w_out, *, tm):
    m, d = h.shape
    c = ysb.shape[1]
    nb = w_branch.shape[0]
    assert nb == 3 and gate_col0 % d == 0
    row = lambda w: pl.BlockSpec((tm, w), lambda i: (i, 0))
    gate_w = lambda n: pl.BlockSpec((d, d), lambda i, n=n: (0, gate_col0 // d + n))
    return pl.pallas_call(
        _merge_kernel,
        grid=(m // tm,),
        in_specs=[
            row(c), row(c), row(c), row(d),
            pl.BlockSpec((1, d), lambda i: (0, 0)),
            gate_w(0), gate_w(1), gate_w(2),
            pl.BlockSpec((1, nb * d), lambda i: (0, 0)),
            pl.BlockSpec((nb, c, d), lambda i: (0, 0, 0)),
            pl.BlockSpec((d, d), lambda i: (0, 0)),
        ],
        out_specs=row(d),
        out_shape=jax.ShapeDtypeStruct((m, d), F32),
        compiler_params=_params("parallel"),
        name="merge",
    )(ysb, ycv, ysc, h, g.reshape(1, d), w_in, w_in, w_in, b_gate.reshape(1, nb * d), w_branch, w_out)


def _xattn_kernel(h_ref, g_ref, wq_ref, kv_ref, wo_ref, o_ref):
    x = h_ref[...]
    d = x.shape[1]
    hd = d // XA_HEADS
    q = _dot(_rms(x, g_ref[...]).astype(BF16), wq_ref[...]).astype(BF16)
    outs = []
    for n in range(XA_HEADS):
        kh = kv_ref[:, n * hd:(n + 1) * hd]
        vh = kv_ref[:, d + n * hd:d + (n + 1) * hd]
        s = lax.dot_general(q[:, n * hd:(n + 1) * hd], kh, _NT, preferred_element_type=F32)
        s = s * (hd ** -0.5)
        e = jnp.exp(s - jnp.max(s, axis=-1, keepdims=True))
        a = e / jnp.sum(e, axis=-1, keepdims=True)
        outs.append(_dot(a.astype(BF16), vh).astype(BF16))
    o_ref[...] = x + _dot(jnp.concatenate(outs, axis=1), wo_ref[...])


def _cross_attention(h, g, wq, kv, wo, *, tm):
    b, s, d = h.shape
    n_mem = kv.shape[1]
    tm = min(tm, s)
    return pl.pallas_call(
        _xattn_kernel,
        grid=(b, s // tm),
        in_specs=[
            pl.BlockSpec((None, tm, d), lambda bi, i: (bi, i, 0)),
            pl.BlockSpec((1, d), lambda bi, i: (0, 0)),
            pl.BlockSpec((d, d), lambda bi, i: (0, 0)),
            pl.BlockSpec((None, n_mem, 2 * d), lambda bi, i: (bi, 0, 0)),
            pl.BlockSpec((d, d), lambda bi, i: (0, 0)),
        ],
        out_specs=pl.BlockSpec((None, tm, d), lambda bi, i: (bi, i, 0)),
        out_shape=jax.ShapeDtypeStruct((b, s, d), F32),
        compiler_params=_params("parallel", "parallel"),
        name="cross_attention",
    )(h, g.reshape(1, d), wq, kv, wo)


_KEY_POS = float(N_KEYS * N_KEYS)
_KEY_BIG = 2.0 ** 30

_SORT16 = (
    (0, 13), (1, 12), (2, 15), (3, 14), (4, 8), (5, 6), (7, 11), (9, 10),
    (0, 5), (1, 7), (2, 9), (3, 4), (6, 13), (8, 14), (10, 15), (11, 12),
    (0, 1), (2, 3), (4, 5), (6, 8), (7, 9), (10, 11), (12, 13), (14, 15),
    (0, 2), (1, 3), (4, 10), (5, 11), (6, 7), (8, 9), (12, 14), (13, 15),
    (1, 2), (3, 12), (4, 6), (5, 7), (8, 10), (9, 11), (13, 14),
    (1, 4), (2, 6), (5, 8), (7, 10), (9, 13), (11, 14),
    (2, 4), (3, 6), (9, 12), (11, 13),
    (3, 5), (6, 8), (7, 9), (10, 12),
    (3, 4), (5, 6), (7, 8), (9, 10), (11, 12),
    (6, 7), (8, 9),
)


def _sorted_topk(scores, k):
    depth = scores.shape[0] // SUBLANES
    assert depth == len({i for ce in _SORT16 for i in ce}) and k <= depth
    sub = lax.broadcasted_iota(jnp.int32, (SUBLANES, scores.shape[1]), 0).astype(F32)
    a = [scores[v * SUBLANES:(v + 1) * SUBLANES] for v in range(depth)]
    ix = [sub + float(v * SUBLANES) for v in range(depth)]
    for i, j in _SORT16:
        swap = (a[j] > a[i]) | ((a[j] == a[i]) & (ix[j] < ix[i]))
        a[i], a[j] = jnp.where(swap, a[j], a[i]), jnp.where(swap, a[i], a[j])
        ix[i], ix[j] = jnp.where(swap, ix[j], ix[i]), jnp.where(swap, ix[i], ix[j])
    vals, rows = [], []
    for n in range(k):
        m = jnp.max(a[0], axis=0, keepdims=True)
        row = jnp.min(jnp.where(a[0] == m, ix[0], float(scores.shape[0])), axis=0, keepdims=True)
        vals.append(m)
        rows.append(row)
        won = ix[0] == row
        for d in range(k - 1 - n):
            a[d] = jnp.where(won, a[d + 1], a[d])
            ix[d] = jnp.where(won, ix[d + 1], ix[d])
    return vals, rows


def _merge_pair_sums(v1, i1, v2, i2, k):
    irow = lax.broadcasted_iota(jnp.int32, v1.shape, 0)
    sums, keys = [], []
    for j in range(k):
        live = irow < k // (j + 1)
        sums.append(jnp.where(live, v1 + v2[j:j + 1], -jnp.inf))
        pos = (irow * k + j).astype(F32)
        keys.append(pos * _KEY_POS + i1 * float(N_KEYS) + i2[j:j + 1])
    tops, top_keys = [], []
    for n in range(k):
        m = jnp.max(sums[0], axis=0, keepdims=True)
        key = jnp.min(jnp.where(sums[0] == m, keys[0], _KEY_BIG), axis=0, keepdims=True)
        tops.append(m)
        top_keys.append(key)
        won = keys[0] == key
        for d in range(k - 1 - n):
            sums[d] = jnp.where(won, sums[d + 1], sums[d])
            keys[d] = jnp.where(won, keys[d + 1], keys[d])
    return tops, top_keys


def _route_kernel(h_ref, g_ref, wq_ref, keys_ref, hn_ref, e1_ref, e2_ref, gate_ref,
                  q_scr, v_scr, i_scr, top_scr, key_scr, e1t_scr, e2t_scr, gt_scr, *, tm):
    k = PEER_TOPK
    hn = _rms(h_ref[...], g_ref[...]).astype(BF16)
    hn_ref[...] = hn
    q_scr[...] = _dot(hn, wq_ref[...]).astype(BF16)

    def one_head(h, slot):
        for p in range(2):
            hp = h * 2 + p
            qs = q_scr[:, pl.ds(pl.multiple_of(hp * N_KEYS, N_KEYS), N_KEYS)]
            s = lax.dot_general(keys_ref[hp], qs, _NT, preferred_element_type=F32)
            for t0 in range(0, tm, LANES):
                vals, rows = _sorted_topk(s[:, t0:t0 + LANES], k)
                for n in range(k):
                    v_scr[slot, p, n:n + 1, t0:t0 + LANES] = vals[n]
                    i_scr[slot, p, n:n + 1, t0:t0 + LANES] = rows[n]
        for t0 in range(0, tm, LANES):
            lanes = slice(t0, t0 + LANES)
            tops, keys = _merge_pair_sums(v_scr[slot, 0, :, lanes], i_scr[slot, 0, :, lanes],
                                          v_scr[slot, 1, :, lanes], i_scr[slot, 1, :, lanes], k)
            for n in range(k):
                top_scr[slot, n:n + 1, lanes] = tops[n]
                key_scr[slot, n:n + 1, lanes] = keys[n]
        top = top_scr[slot]
        kk = key_scr[slot]
        e = kk - jnp.floor(kk * (1.0 / _KEY_POS)) * _KEY_POS
        e1 = jnp.floor(e * (1.0 / N_KEYS))
        e2 = e - e1 * float(N_KEYS)
        ex = jnp.exp(top - top[0:1])
        gate = ex / jnp.sum(ex, axis=0, keepdims=True)
        r0 = pl.multiple_of(h * k, k)
        e1t_scr[pl.ds(r0, k), :] = e1
        e2t_scr[pl.ds(r0, k), :] = e2
        gt_scr[pl.ds(r0, k), :] = gate

    def head_group(n, carry):
        for slot in range(ROUTE_HEADS_PER_TRIP):
            one_head(ROUTE_HEADS_PER_TRIP * n + slot, slot)
        return carry

    lax.fori_loop(0, PEER_HEADS // ROUTE_HEADS_PER_TRIP, head_group, 0)
    e1_ref[...] = e1t_scr[...].T.astype(jnp.int32)
    e2_ref[...] = e2t_scr[...].T.astype(jnp.int32)
    gate_ref[...] = gt_scr[...].T


def _peer_route(h, g, wq, keys, *, tm):
    m, d = h.shape
    qw = wq.shape[1]
    nsel = PEER_HEADS * PEER_TOPK
    k = PEER_TOPK
    kern = functools.partial(_route_kernel, tm=tm)
    row = lambda w: pl.BlockSpec((tm, w), lambda i: (i, 0))
    return pl.pallas_call(
        kern,
        grid=(m // tm,),
        in_specs=[
            row(d),
            pl.BlockSpec((1, d), lambda i: (0, 0)),
            pl.BlockSpec((d, qw), lambda i: (0, 0)),
            pl.BlockSpec(keys.shape, lambda i: (0, 0, 0)),
        ],
        out_specs=[row(d), row(nsel), row(nsel), row(nsel)],
        out_shape=[
            jax.ShapeDtypeStruct((m, d), BF16),
            jax.ShapeDtypeStruct((m, nsel), jnp.int32),
            jax.ShapeDtypeStruct((m, nsel), jnp.int32),
            jax.ShapeDtypeStruct((m, nsel), F32),
        ],
        scratch_shapes=[
            pltpu.VMEM((tm, qw), BF16),
            pltpu.VMEM((ROUTE_HEADS_PER_TRIP, 2, k, tm), F32),
            pltpu.VMEM((ROUTE_HEADS_PER_TRIP, 2, k, tm), F32),
            pltpu.VMEM((ROUTE_HEADS_PER_TRIP, k, tm), F32),
            pltpu.VMEM((ROUTE_HEADS_PER_TRIP, k, tm), F32),
            pltpu.VMEM((nsel, tm), F32),
            pltpu.VMEM((nsel, tm), F32),
            pltpu.VMEM((nsel, tm), F32),
        ],
        compiler_params=_params("parallel"),
        name="peer_route",
    )(h, g.reshape(1, d), wq, keys)


def _asel_kernel(hn_ref, ut_ref, e1_ref, e2_ref, o_ref, a0_scr, a1_scr, *, chunks):
    grp = pl.program_id(1)

    @pl.when(grp == 0)
    def _():
        o_ref[...] = jnp.zeros(o_ref.shape, F32)

    hn = hn_ref[...]
    e1 = e1_ref[...]
    e2 = e2_ref[...]
    per_dot = DOT_COLS // N_KEYS
    nblk = chunks // per_dot
    assert nblk % 2 == 0

    def matmul(n, dst):
        cols = pl.ds(pl.multiple_of(n * DOT_COLS, DOT_COLS), DOT_COLS)
        dst[...] = _dot(hn, ut_ref[:, cols])

    def gather(n, src, sel):
        for c in range(per_dot):
            gathered = jnp.take_along_axis(src[:, c * N_KEYS:(c + 1) * N_KEYS], e2, axis=1)
            sel = jnp.where(e1 == grp * chunks + n * per_dot + c, gathered, sel)
        return sel

    def body(m, sel):
        matmul(2 * m + 1, a1_scr)
        sel = gather(2 * m, a0_scr, sel)
        matmul(2 * m + 2, a0_scr)
        return gather(2 * m + 1, a1_scr, sel)

    matmul(0, a0_scr)
    sel = lax.fori_loop(0, nblk // 2 - 1, body, o_ref[...])
    matmul(nblk - 1, a1_scr)
    sel = gather(nblk - 2, a0_scr, sel)
    o_ref[...] = gather(nblk - 1, a1_scr, sel)


def _peer_asel(hn, ut, e1, e2, *, tm, te):
    m, d = hn.shape
    ne = ut.shape[1]
    nsel = e1.shape[1]
    kern = functools.partial(_asel_kernel, chunks=te // N_KEYS)
    return pl.pallas_call(
        kern,
        grid=(m // tm, ne // te),
        in_specs=[
            pl.BlockSpec((tm, d), lambda i, j: (i, 0)),
            pl.BlockSpec((d, te), lambda i, j: (0, j)),
            pl.BlockSpec((tm, nsel), lambda i, j: (i, 0)),
            pl.BlockSpec((tm, nsel), lambda i, j: (i, 0)),
        ],
        out_specs=pl.BlockSpec((tm, nsel), lambda i, j: (i, 0)),
        out_shape=jax.ShapeDtypeStruct((m, nsel), F32),
        scratch_shapes=[pltpu.VMEM((tm, DOT_COLS), F32)] * 2,
        compiler_params=_params("parallel", "arbitrary"),
        name="peer_asel",
    )(hn, ut, e1, e2)


def _pout_kernel(h_ref, asel_ref, gate_ref, e1_ref, e2_ref, v_ref, o_ref, w_scr, w3_scr, g0_scr,
                 g1_scr, *, tm, chunks):
    grp = pl.program_id(1)

    @pl.when(grp == 0)
    def _():
        a = asel_ref[...]
        gelu = 0.5 * a * (1.0 + lax.erf(a * (2.0 ** -0.5)))
        w_scr[...] = gate_ref[...] * gelu
        o_ref[...] = h_ref[...]
        sub = lax.broadcasted_iota(jnp.int32, (N_KEYS, e1_ref.shape[1]), 0)

        def tiles(n, dst):
            t0 = pl.multiple_of(n * W_GROUP, W_GROUP)
            e1g = e1_ref[pl.ds(t0, W_GROUP), :]
            e2g = e2_ref[pl.ds(t0, W_GROUP), :]
            wg = w_scr[pl.ds(t0, W_GROUP), :]
            for t in range(W_GROUP):
                lhs = jnp.where(sub == e1g[t:t + 1], wg[t:t + 1], 0.0).astype(BF16)
                rhs = jnp.where(sub == e2g[t:t + 1], 1.0, 0.0).astype(BF16)
                dst[t * N_KEYS:(t + 1) * N_KEYS, :] = lax.dot_general(
                    lhs, rhs, _NT, preferred_element_type=F32)

        def regroup(n, src):
            t0 = pl.multiple_of(n * W_GROUP, W_GROUP)
            g = src[...].astype(BF16).reshape(W_GROUP, N_KEYS, N_KEYS)
            w3_scr[:, pl.ds(t0, W_GROUP), :] = jnp.swapaxes(g, 0, 1)

        groups = tm // W_GROUP
        assert groups % 2 == 0

        def body(m, carry):
            tiles(2 * m + 1, g1_scr)
            regroup(2 * m, g0_scr)
            tiles(2 * m + 2, g0_scr)
            regroup(2 * m + 1, g1_scr)
            return carry

        tiles(0, g0_scr)
        lax.fori_loop(0, groups // 2 - 1, body, 0)
        tiles(groups - 1, g1_scr)
        regroup(groups - 2, g0_scr)
        regroup(groups - 1, g1_scr)

    parts = [w3_scr[grp * chunks + c] for c in range(chunks)]
    o_ref[...] += _dot(jnp.concatenate(parts, axis=1), v_ref[...])


def _peer_out(h, asel, gate, e1, e2, v, *, tm, te):
    m, d = h.shape
    ne = v.shape[0]
    nsel = e1.shape[1]
    kern = functools.partial(_pout_kernel, tm=tm, chunks=te // N_KEYS)
    sel = pl.BlockSpec((tm, nsel), lambda i, j: (i, 0))
    return pl.pallas_call(
        kern,
        grid=(m // tm, ne // te),
        in_specs=[
            pl.BlockSpec((tm, d), lambda i, j: (i, 0)),
            sel, sel, sel, sel,
            pl.BlockSpec((te, d), lambda i, j: (j, 0)),
        ],
        out_specs=pl.BlockSpec((tm, d), lambda i, j: (i, 0)),
        out_shape=jax.ShapeDtypeStruct((m, d), F32),
        scratch_shapes=[pltpu.VMEM((tm, nsel), F32), pltpu.VMEM((N_KEYS, tm, N_KEYS), BF16),
                        pltpu.VMEM((W_GROUP * N_KEYS, N_KEYS), F32),
                        pltpu.VMEM((W_GROUP * N_KEYS, N_KEYS), F32)],
        compiler_params=_params("parallel", "arbitrary"),
        name="peer_out",
    )(h, asel, gate, e1, e2, v)


def _final_norm_kernel(x_ref, g_ref, o_ref):
    o_ref[...] = _rms(x_ref[...], g_ref[...])


def _final_norm(h, g, *, tm):
    m, d = h.shape
    return pl.pallas_call(
        _final_norm_kernel,
        grid=(m // tm,),
        in_specs=[pl.BlockSpec((tm, d), lambda i: (i, 0)), pl.BlockSpec((1, d), lambda i: (0, 0))],
        out_specs=pl.BlockSpec((tm, d), lambda i: (i, 0)),
        out_shape=jax.ShapeDtypeStruct((m, d), F32),
        compiler_params=_params("parallel"),
        name="final_norm",
    )(h, g.reshape(1, d))


def kernel(x, mem, norm_mix_g, w_in, conv_dw_w, conv_dw_b, conv_ln_g, conv_ln_b, sc_w, w_branch, b_gate, w_out, norm_xa_g, norm_mem_g, xa_wq, xa_wkv, xa_wo, norm_ffn_g, peer_wq, peer_keys, peer_u, peer_v, final_g):
    b, s, d = x.shape
    depth = w_in.shape[0]
    n_mem = mem.shape[1]
    sb_w = w_branch.shape[2]
    cv_w = conv_dw_w.shape[2]
    nb = w_branch.shape[1]
    qkv_w, conv_w = 3 * sb_w, 5 * cv_w
    m = b * s
    tm_big = min(1024, m)

    h = x.reshape(m, d)
    mem2 = mem.reshape(b * n_mem, d)
    for l in range(depth):
        w_in_l = w_in[l].astype(BF16)
        p_qkv = _norm_matmul(h, norm_mix_g[l], w_in_l, col0=0, ncols=qkv_w, out_dtype=BF16,
                             tm=tm_big, tn=512, name="in_proj_qkv")
        y_sb = _sb_attention(p_qkv.reshape(b, s, qkv_w), width=sb_w, bq=512, bk=256)
        y_cv, y_sc = _conv_branches(h.reshape(b, s, d), norm_mix_g[l], w_in_l, qkv_w, conv_dw_w[l],
                                    conv_dw_b[l], conv_ln_g[l], conv_ln_b[l], sc_w[l], ts=256)
        h = _merge(y_sb.reshape(m, sb_w), y_cv.reshape(m, cv_w), y_sc.reshape(m, cv_w), h,
                   norm_mix_g[l], w_in_l, qkv_w + conv_w, b_gate[l], w_branch[l].astype(BF16),
                   w_out[l].astype(BF16), tm=256)

        kv = _norm_matmul(mem2, norm_mem_g[l], xa_wkv[l].astype(BF16), col0=0, ncols=2 * d,
                          out_dtype=BF16, tm=min(1024, b * n_mem), tn=512, name="mem_kv")
        h = _cross_attention(h.reshape(b, s, d), norm_xa_g[l], xa_wq[l].astype(BF16),
                             kv.reshape(b, n_mem, 2 * d), xa_wo[l].astype(BF16), tm=256).reshape(m, d)

        keys = peer_keys[l].astype(BF16).reshape(PEER_HEADS * 2, N_KEYS, peer_keys.shape[-1])
        hn, e1, e2, gate = _peer_route(h, norm_ffn_g[l], peer_wq[l].astype(BF16), keys, tm=256)
        asel = _peer_asel(hn, peer_u[l].T.astype(BF16), e1, e2, tm=min(512, m), te=8192)
        h = _peer_out(h, asel, gate, e1, e2, peer_v[l].astype(BF16), tm=min(512, m), te=4096)
    return _final_norm(h, final_g, tm=tm_big).reshape(b, s, d)
```

```python
import functools

import jax
import jax.numpy as jnp
from jax import lax
from jax.experimental import pallas as pl
from jax.experimental.pallas import tpu as pltpu

F32 = jnp.float32
BF16 = jnp.bfloat16

LANES = 128
SUBLANES = 8
MXU_WIDTH = 256
DOT_COLS = 2 * MXU_WIDTH
VMEM_LIMIT_BYTES = 56 * 1024 * 1024

SB_HEAD_DIM = 64
SB_HEADS_PER_TILE = LANES // SB_HEAD_DIM
XA_HEADS = 4
PEER_HEADS = 8
PEER_TOPK = 16
N_KEYS = 128
CONV_HALO = 32
ROUTE_HEADS_PER_TRIP = 4
W_GROUP = 16

_NT = (((1,), (1,)), ((), ()))


def _params(*sem):
    return pltpu.CompilerParams(dimension_semantics=sem, vmem_limit_bytes=VMEM_LIMIT_BYTES)


def _rms(x, g, eps=1e-6):
    return x * lax.rsqrt(jnp.mean(x * x, axis=-1, keepdims=True) + eps) * g


def _dot(a, b):
    return jnp.dot(a, b, preferred_element_type=F32)


def _norm_matmul_kernel(x_ref, g_ref, w_ref, o_ref, xn_ref):
    @pl.when(pl.program_id(1) == 0)
    def _():
        xn_ref[...] = _rms(x_ref[...], g_ref[...]).astype(BF16)

    o_ref[...] = _dot(xn_ref[...], w_ref[...]).astype(o_ref.dtype)


def _norm_matmul(x, g, w, *, col0, ncols, out_dtype, tm, tn, name):
    m, k = x.shape
    off = col0 // tn
    assert col0 % tn == 0 and ncols % tn == 0 and m % tm == 0
    return pl.pallas_call(
        _norm_matmul_kernel,
        grid=(m // tm, ncols // tn),
        in_specs=[
            pl.BlockSpec((tm, k), lambda i, j: (i, 0)),
            pl.BlockSpec((1, k), lambda i, j: (0, 0)),
            pl.BlockSpec((k, tn), lambda i, j: (0, j + off)),
        ],
        out_specs=pl.BlockSpec((tm, tn), lambda i, j: (i, j)),
        out_shape=jax.ShapeDtypeStruct((m, ncols), out_dtype),
        scratch_shapes=[pltpu.VMEM((tm, k), BF16)],
        compiler_params=_params("parallel", "arbitrary"),
        name=name,
    )(x, g.reshape(1, k), w)


def _softplus(z):
    return jnp.maximum(z, 0.0) + jnp.log(1.0 + jnp.exp(-jnp.abs(z)))


def _sb_kernel(q_ref, k_ref, v_ref, o_ref, *, bq, bk, scale):
    i = pl.program_id(2)
    per_q = bq // bk
    q = q_ref[...]
    lane = lax.broadcasted_iota(jnp.int32, (bq, LANES), 1)
    trow = lax.broadcasted_iota(jnp.int32, (bk, bk), 0)
    tcol = lax.broadcasted_iota(jnp.int32, (bk, bk), 1)
    tri = jnp.where(trow > tcol, 1.0, 0.0).astype(BF16)
    tri2 = jnp.concatenate([tri, tri], axis=0)
    zero_q = jnp.zeros_like(q)
    heads = range(SB_HEADS_PER_TILE)
    qs = q * jnp.asarray(scale, q.dtype)
    qhs = [jnp.where((lane // SB_HEAD_DIM) == hh, qs, zero_q) for hh in heads]

    def step(carry, blocks):
        chains = [(h, n) for h in heads for n in range(len(blocks))]
        ks = [k_ref[pl.ds(pl.multiple_of(j * bk, bk), bk), :] for j, _, _ in blocks]
        vs = [v_ref[pl.ds(pl.multiple_of(j * bk, bk), bk), :] for j, _, _ in blocks]
        z = {(h, n): lax.dot_general(qhs[h][blocks[n][1]:], ks[n], _NT, preferred_element_type=F32)
             for h, n in chains}
        keep, lb, lk, split = {}, {}, {}, {}
        for ch in chains:
            sp = _softplus(z[ch])
            if blocks[ch[1]][2]:
                keep[ch] = (lax.broadcasted_iota(jnp.int32, sp.shape, 1)
                            < lax.broadcasted_iota(jnp.int32, sp.shape, 0))
                lk[ch] = jnp.where(keep[ch], -sp, 0.0)
            else:
                lk[ch] = -sp
            lb[ch] = z[ch] - sp
            hi = lk[ch].astype(BF16)
            lo = (lk[ch] - hi.astype(F32)).astype(BF16)
            split[ch] = jnp.concatenate([hi, lo], axis=1)
        after = {ch: _dot(split[ch], tri2) for ch in chains}
        mass = {ch: jnp.sum(lk[ch], axis=1, keepdims=True) for ch in chains}
        out = []
        for h in heads:
            acc, c = carry[h]
            for n, (_, r0, causal) in enumerate(blocks):
                a = jnp.exp(lb[h, n] + after[h, n] + c[r0:])
                if causal:
                    a = jnp.where(keep[h, n], a, 0.0)
                da, dc = _dot(a.astype(BF16), vs[n]), mass[h, n]
                if r0:
                    da = jnp.concatenate([jnp.zeros((r0, LANES), F32), da], axis=0)
                    dc = jnp.concatenate([jnp.zeros((r0, 1), F32), dc], axis=0)
                acc, c = acc + da, c + dc
            out.append((acc, c))
        return tuple(out)

    zero = (jnp.zeros((bq, LANES), F32), jnp.zeros((bq, 1), F32))
    carry = step(tuple(zero for _ in heads),
                 [(i * per_q + d, d * bk, True) for d in reversed(range(per_q))])
    n_left = i * per_q
    carry = lax.fori_loop(
        0, n_left // 2,
        lambda n, carry: step(
            carry, [(n_left - 1 - 2 * n, 0, False), (n_left - 2 - 2 * n, 0, False)]), carry)
    if per_q % 2:
        carry = lax.cond(n_left % 2 == 1, lambda carry: step(carry, [(0, 0, False)]),
                         lambda carry: carry, carry)
    o = carry[0][0]
    for hh in heads[1:]:
        o = jnp.where((lane // SB_HEAD_DIM) == hh, carry[hh][0], o)
    o_ref[...] = o.astype(o_ref.dtype)


def _sb_attention(p_qkv, *, width, bq, bk):
    b, s, _ = p_qkv.shape
    tiles = width // LANES
    bq, bk = min(bq, s), min(bk, s)
    assert bq % bk == 0 and s % bq == 0
    kern = functools.partial(_sb_kernel, bq=bq, bk=bk, scale=SB_HEAD_DIM ** -0.5)
    return pl.pallas_call(
        kern,
        grid=(b, tiles, s // bq),
        in_specs=[
            pl.BlockSpec((None, bq, LANES), lambda bi, t, i: (bi, i, t)),
            pl.BlockSpec((None, s, LANES), lambda bi, t, i: (bi, 0, tiles + t)),
            pl.BlockSpec((None, s, LANES), lambda bi, t, i: (bi, 0, 2 * tiles + t)),
        ],
        out_specs=pl.BlockSpec((None, bq, LANES), lambda bi, t, i: (bi, i, t)),
        out_shape=jax.ShapeDtypeStruct((b, s, width), BF16),
        compiler_params=_params("parallel", "parallel", "arbitrary"),
        name="sb_attention",
    )(p_qkv, p_qkv, p_qkv)


def _conv_kernel(h_ref, g_ref, w_val, w_gate, w_gb, w_gc, w_xin, dww_ref, dwb_ref, lng_ref, lnb_ref,
                 scw_ref, ycv_ref, ysc_ref, ubuf, sbuf, gb_scr, win_scr, *, ts, rows):
    t = pl.program_id(1)
    c = ubuf.shape[1]
    xn = _rms(h_ref[...], g_ref[...]).astype(BF16)

    @pl.when(t == 0)
    def _():
        ubuf[0:CONV_HALO, :] = jnp.zeros((CONV_HALO, c), F32)
        sbuf[0:CONV_HALO, :] = jnp.zeros((CONV_HALO, c), F32)

    @pl.when(t > 0)
    def _():
        ubuf[0:CONV_HALO, :] = ubuf[ts:ts + CONV_HALO, :]
        sbuf[0:CONV_HALO, :] = sbuf[ts:ts + CONV_HALO, :]

    ubuf[CONV_HALO:, :] = _dot(xn, w_val[...]) * jax.nn.sigmoid(_dot(xn, w_gate[...]))
    sbuf[CONV_HALO:, :] = _dot(xn, w_gc[...]) * _dot(xn, w_xin[...])
    gb_scr[...] = _dot(xn, w_gb[...])
    kw = dww_ref.shape[0]
    sw = scw_ref.shape[0]

    first = CONV_HALO - (kw - 1)
    for r0 in range(0, ts, rows):
        acc = jnp.zeros((rows, c), F32)
        for phase in range(SUBLANES):
            taps = [j for j in range(kw) if (first + j) % SUBLANES == phase]
            if not taps:
                continue
            lo, hi = first + taps[0], first + taps[-1]
            span = hi - lo + rows
            win_scr[phase, 0:span, :] = ubuf[r0 + lo:r0 + lo + span, :]
            for j in taps:
                off = first + j - lo
                acc = acc + dww_ref[j:j + 1, :] * win_scr[phase, off:off + rows, :]
        acc = acc + dwb_ref[...]
        mu = jnp.mean(acc, axis=-1, keepdims=True)
        d = acc - mu
        var = jnp.mean(d * d, axis=-1, keepdims=True)
        y = d * lax.rsqrt(var + 1e-5) * lng_ref[...] + lnb_ref[...]
        ycv_ref[r0:r0 + rows, :] = (y * jax.nn.sigmoid(y)).astype(ycv_ref.dtype)
        acc2 = jnp.zeros((rows, c), F32)
        for j in range(sw):
            lo = r0 + CONV_HALO - (sw - 1) + j
            acc2 = acc2 + scw_ref[j:j + 1, :] * sbuf[lo:lo + rows, :]
        ysc_ref[r0:r0 + rows, :] = (gb_scr[r0:r0 + rows, :] * acc2).astype(ysc_ref.dtype)


def _conv_branches(h, g, w_in, conv_col0, dww, dwb, lng, lnb, scw, *, ts):
    b, s, d = h.shape
    c = dww.shape[1]
    ts = min(ts, s)
    assert dww.shape[0] - 1 <= CONV_HALO and scw.shape[0] - 1 <= CONV_HALO and conv_col0 % c == 0
    rows = 64
    kern = functools.partial(_conv_kernel, ts=ts, rows=rows)
    w_col = lambda n: pl.BlockSpec((d, c), lambda bi, t, n=n: (0, conv_col0 // c + n))
    full = lambda a: pl.BlockSpec(a.shape, lambda bi, t: (0, 0))
    vecs = [dww, dwb.reshape(1, c), lng.reshape(1, c), lnb.reshape(1, c), scw]
    out_spec = pl.BlockSpec((None, ts, c), lambda bi, t: (bi, t, 0))
    return pl.pallas_call(
        kern,
        grid=(b, s // ts),
        in_specs=[pl.BlockSpec((None, ts, d), lambda bi, t: (bi, t, 0)),
                  pl.BlockSpec((1, d), lambda bi, t: (0, 0))]
        + [w_col(n) for n in range(5)] + [full(a) for a in vecs],
        out_specs=[out_spec, out_spec],
        out_shape=[jax.ShapeDtypeStruct((b, s, c), BF16)] * 2,
        scratch_shapes=[pltpu.VMEM((CONV_HALO + ts, c), F32)] * 2 + [
            pltpu.VMEM((ts, c), F32), pltpu.VMEM((SUBLANES, CONV_HALO + rows, c), F32)],
        compiler_params=_params("parallel", "arbitrary"),
        name="conv_branches",
    )(h, g.reshape(1, d), w_in, w_in, w_in, w_in, w_in, *vecs)


def _merge_kernel(ysb_ref, ycv_ref, ysc_ref, h_ref, g_ref, wg0_ref, wg1_ref, wg2_ref, bg_ref,
                  wb_ref, wo_ref, o_ref):
    h = h_ref[...]
    d = h.shape[1]
    xn = _rms(h, g_ref[...]).astype(BF16)
    merged = jnp.zeros(h.shape, F32)
    branches = ((ysb_ref, wg0_ref), (ycv_ref, wg1_ref), (ysc_ref, wg2_ref))
    for n, (y_ref, wg_ref) in enumerate(branches):
        proj = _dot(y_ref[...], wb_ref[n])
        gate = jax.nn.sigmoid(_dot(xn, wg_ref[...]) + bg_ref[:, n * d:(n + 1) * d])
        merged = merged + gate * proj
    o_ref[...] = h + _dot(merged.astype(BF16), wo_ref[...])


def _merge(ysb, ycv, ysc, h, g, w_in, gate_col0, b_gate, w_branch, w_out, *, tm):
    m, d = h.shape
    c = ysb.shape[1]
    nb = w_branch.shape[0]
    assert nb == 3 and gate_col0 % d == 0
    row = lambda w: pl.BlockSpec((tm, w), lambda i: (i, 0))
    gate_w = lambda n: pl.BlockSpec((d, d), lambda i, n=n: (0, gate_col0 // d + n))
    return pl.pallas_call(
        _merge_kernel,
        grid=(m // tm,),
        in_specs=[
            row(c), row(c), row(c), row(d),
            pl.BlockSpec((1, d), lambda i: (0, 0)),
            gate_w(0), gate_w(1), gate_w(2),
            pl.BlockSpec((1, nb * d), lambda i: (0, 0)),
            pl.BlockSpec((nb, c, d), lambda i: (0, 0, 0)),
            pl.BlockSpec((d, d), lambda i: (0, 0)),
        ],
        out_specs=row(d),
        out_shape=jax.ShapeDtypeStruct((m, d), F32),
        compiler_params=_params("parallel"),
        name="merge",
    )(ysb, ycv, ysc, h, g.reshape(1, d), w_in, w_in, w_in, b_gate.reshape(1, nb * d), w_branch, w_out)


def _xattn_kernel(h_ref, g_ref, wq_ref, kv_ref, wo_ref, o_ref):
    x = h_ref[...]
    d = x.shape[1]
    hd = d // XA_HEADS
    q = _dot(_rms(x, g_ref[...]).astype(BF16), wq_ref[...]).astype(BF16)
    outs = []
    for n in range(XA_HEADS):
        kh = kv_ref[:, n * hd:(n + 1) * hd]
        vh = kv_ref[:, d + n * hd:d + (n + 1) * hd]
        s = lax.dot_general(q[:, n * hd:(n + 1) * hd], kh, _NT, preferred_element_type=F32)
        s = s * (hd ** -0.5)
        e = jnp.exp(s - jnp.max(s, axis=-1, keepdims=True))
        a = e / jnp.sum(e, axis=-1, keepdims=True)
        outs.append(_dot(a.astype(BF16), vh).astype(BF16))
    o_ref[...] = x + _dot(jnp.concatenate(outs, axis=1), wo_ref[...])


def _cross_attention(h, g, wq, kv, wo, *, tm):
    b, s, d = h.shape
    n_mem = kv.shape[1]
    tm = min(tm, s)
    return pl.pallas_call(
        _xattn_kernel,
        grid=(b, s // tm),
        in_specs=[
            pl.BlockSpec((None, tm, d), lambda bi, i: (bi, i, 0)),
            pl.BlockSpec((1, d), lambda bi, i: (0, 0)),
            pl.BlockSpec((d, d), lambda bi, i: (0, 0)),
            pl.BlockSpec((None, n_mem, 2 * d), lambda bi, i: (bi, 0, 0)),
            pl.BlockSpec((d, d), lambda bi, i: (0, 0)),
        ],
        out_specs=pl.BlockSpec((None, tm, d), lambda bi, i: (bi, i, 0)),
        out_shape=jax.ShapeDtypeStruct((b, s, d), F32),
        compiler_params=_params("parallel", "parallel"),
        name="cross_attention",
    )(h, g.reshape(1, d), wq, kv, wo)


_KEY_POS = float(N_KEYS * N_KEYS)
_KEY_BIG = 2.0 ** 30

_SORT16 = (
    (0, 13), (1, 12), (2, 15), (3, 14), (4, 8), (5, 6), (7, 11), (9, 10),
    (0, 5), (1, 7), (2, 9), (3, 4), (6, 13), (8, 14), (10, 15), (11, 12),
    (0, 1), (2, 3), (4, 5), (6, 8), (7, 9), (10, 11), (12, 13), (14, 15),
    (0, 2), (1, 3), (4, 10), (5, 11), (6, 7), (8, 9), (12, 14), (13, 15),
    (1, 2), (3, 12), (4, 6), (5, 7), (8, 10), (9, 11), (13, 14),
    (1, 4), (2, 6), (5, 8), (7, 10), (9, 13), (11, 14),
    (2, 4), (3, 6), (9, 12), (11, 13),
    (3, 5), (6, 8), (7, 9), (10, 12),
    (3, 4), (5, 6), (7, 8), (9, 10), (11, 12),
    (6, 7), (8, 9),
)


def _sorted_topk(scores, k):
    depth = scores.shape[0] // SUBLANES
    assert depth == len({i for ce in _SORT16 for i in ce}) and k <= depth
    sub = lax.broadcasted_iota(jnp.int32, (SUBLANES, scores.shape[1]), 0).astype(F32)
    a = [scores[v * SUBLANES:(v + 1) * SUBLANES] for v in range(depth)]
    ix = [sub + float(v * SUBLANES) for v in range(depth)]
    for i, j in _SORT16:
        swap = (a[j] > a[i]) | ((a[j] == a[i]) & (ix[j] < ix[i]))
        a[i], a[j] = jnp.where(swap, a[j], a[i]), jnp.where(swap, a[i], a[j])
        ix[i], ix[j] = jnp.where(swap, ix[j], ix[i]), jnp.where(swap, ix[i], ix[j])
    vals, rows = [], []
    for n in range(k):
        m = jnp.max(a[0], axis=0, keepdims=True)
        row = jnp.min(jnp.where(a[0] == m, ix[0], float(scores.shape[0])), axis=0, keepdims=True)
        vals.append(m)
        rows.append(row)
        won = ix[0] == row
        for d in range(k - 1 - n):
            a[d] = jnp.where(won, a[d + 1], a[d])
            ix[d] = jnp.where(won, ix[d + 1], ix[d])
    return vals, rows


def _merge_pair_sums(v1, i1, v2, i2, k):
    irow = lax.broadcasted_iota(jnp.int32, v1.shape, 0)
    sums, keys = [], []
    for j in range(k):
        live = irow < k // (j + 1)
        sums.append(jnp.where(live, v1 + v2[j:j + 1], -jnp.inf))
        pos = (irow * k + j).astype(F32)
        keys.append(pos * _KEY_POS + i1 * float(N_KEYS) + i2[j:j + 1])
    tops, top_keys = [], []
    for n in range(k):
        m = jnp.max(sums[0], axis=0, keepdims=True)
        key = jnp.min(jnp.where(sums[0] == m, keys[0], _KEY_BIG), axis=0, keepdims=True)
        tops.append(m)
        top_keys.append(key)
        won = keys[0] == key
        for d in range(k - 1 - n):
            sums[d] = jnp.where(won, sums[d + 1], sums[d])
            keys[d] = jnp.where(won, keys[d + 1], keys[d])
    return tops, top_keys


def _route_kernel(h_ref, g_ref, wq_ref, keys_ref, hn_ref, e1_ref, e2_ref, gate_ref,
                  q_scr, v_scr, i_scr, top_scr, key_scr, e1t_scr, e2t_scr, gt_scr, *, tm):
    k = PEER_TOPK
    hn = _rms(h_ref[...], g_ref[...]).astype(BF16)
    hn_ref[...] = hn
    q_scr[...] = _dot(hn, wq_ref[...]).astype(BF16)

    def one_head(h, slot):
        for p in range(2):
            hp = h * 2 + p
            qs = q_scr[:, pl.ds(pl.multiple_of(hp * N_KEYS, N_KEYS), N_KEYS)]
            s = lax.dot_general(keys_ref[hp], qs, _NT, preferred_element_type=F32)
            for t0 in range(0, tm, LANES):
                vals, rows = _sorted_topk(s[:, t0:t0 + LANES], k)
                for n in range(k):
                    v_scr[slot, p, n:n + 1, t0:t0 + LANES] = vals[n]
                    i_scr[slot, p, n:n + 1, t0:t0 + LANES] = rows[n]
        for t0 in range(0, tm, LANES):
            lanes = slice(t0, t0 + LANES)
            tops, keys = _merge_pair_sums(v_scr[slot, 0, :, lanes], i_scr[slot, 0, :, lanes],
                                          v_scr[slot, 1, :, lanes], i_scr[slot, 1, :, lanes], k)
            for n in range(k):
                top_scr[slot, n:n + 1, lanes] = tops[n]
                key_scr[slot, n:n + 1, lanes] = keys[n]
        top = top_scr[slot]
        kk = key_scr[slot]
        e = kk - jnp.floor(kk * (1.0 / _KEY_POS)) * _KEY_POS
        e1 = jnp.floor(e * (1.0 / N_KEYS))
        e2 = e - e1 * float(N_KEYS)
        ex = jnp.exp(top - top[0:1])
        gate = ex / jnp.sum(ex, axis=0, keepdims=True)
        r0 = pl.multiple_of(h * k, k)
        e1t_scr[pl.ds(r0, k), :] = e1
        e2t_scr[pl.ds(r0, k), :] = e2
        gt_scr[pl.ds(r0, k), :] = gate

    def head_group(n, carry):
        for slot in range(ROUTE_HEADS_PER_TRIP):
            one_head(ROUTE_HEADS_PER_TRIP * n + slot, slot)
        return carry

    lax.fori_loop(0, PEER_HEADS // ROUTE_HEADS_PER_TRIP, head_group, 0)
    e1_ref[...] = e1t_scr[...].T.astype(jnp.int32)
    e2_ref[...] = e2t_scr[...].T.astype(jnp.int32)
    gate_ref[...] = gt_scr[...].T


def _peer_route(h, g, wq, keys, *, tm):
    m, d = h.shape
    qw = wq.shape[1]
    nsel = PEER_HEADS * PEER_TOPK
    k = PEER_TOPK
    kern = functools.partial(_route_kernel, tm=tm)
    row = lambda w: pl.BlockSpec((tm, w), lambda i: (i, 0))
    return pl.pallas_call(
        kern,
        grid=(m // tm,),
        in_specs=[
            row(d),
            pl.BlockSpec((1, d), lambda i: (0, 0)),
            pl.BlockSpec((d, qw), lambda i: (0, 0)),
            pl.BlockSpec(keys.shape, lambda i: (0, 0, 0)),
        ],
        out_specs=[row(d), row(nsel), row(nsel), row(nsel)],
        out_shape=[
            jax.ShapeDtypeStruct((m, d), BF16),
            jax.ShapeDtypeStruct((m, nsel), jnp.int32),
            jax.ShapeDtypeStruct((m, nsel), jnp.int32),
            jax.ShapeDtypeStruct((m, nsel), F32),
        ],
        scratch_shapes=[
            pltpu.VMEM((tm, qw), BF16),
            pltpu.VMEM((ROUTE_HEADS_PER_TRIP, 2, k, tm), F32),
            pltpu.VMEM((ROUTE_HEADS_PER_TRIP, 2, k, tm), F32),
            pltpu.VMEM((ROUTE_HEADS_PER_TRIP, k, tm), F32),
            pltpu.VMEM((ROUTE_HEADS_PER_TRIP, k, tm), F32),
            pltpu.VMEM((nsel, tm), F32),
            pltpu.VMEM((nsel, tm), F32),
            pltpu.VMEM((nsel, tm), F32),
        ],
        compiler_params=_params("parallel"),
        name="peer_route",
    )(h, g.reshape(1, d), wq, keys)


def _asel_kernel(hn_ref, u_ref, e1_ref, e2_ref, o_ref, a0_scr, a1_scr, *, chunks):
    grp = pl.program_id(1)

    @pl.when(grp == 0)
    def _():
        o_ref[...] = jnp.zeros(o_ref.shape, F32)

    hn = hn_ref[...]
    e1 = e1_ref[...]
    e2 = e2_ref[...]
    per_dot = DOT_COLS // N_KEYS
    nblk = chunks // per_dot
    assert nblk % 2 == 0

    def matmul(n, dst):
        rows = pl.ds(pl.multiple_of(n * DOT_COLS, DOT_COLS), DOT_COLS)
        dst[...] = lax.dot_general(hn, u_ref[rows, :], _NT, preferred_element_type=F32)

    def gather(n, src, sel):
        for c in range(per_dot):
            gathered = jnp.take_along_axis(src[:, c * N_KEYS:(c + 1) * N_KEYS], e2, axis=1)
            sel = jnp.where(e1 == grp * chunks + n * per_dot + c, gathered, sel)
        return sel

    def body(m, sel):
        matmul(2 * m + 1, a1_scr)
        sel = gather(2 * m, a0_scr, sel)
        matmul(2 * m + 2, a0_scr)
        return gather(2 * m + 1, a1_scr, sel)

    matmul(0, a0_scr)
    sel = lax.fori_loop(0, nblk // 2 - 1, body, o_ref[...])
    matmul(nblk - 1, a1_scr)
    sel = gather(nblk - 2, a0_scr, sel)
    o_ref[...] = gather(nblk - 1, a1_scr, sel)


def _peer_asel(hn, u, e1, e2, *, tm, te):
    m, d = hn.shape
    ne = u.shape[0]
    nsel = e1.shape[1]
    kern = functools.partial(_asel_kernel, chunks=te // N_KEYS)
    return pl.pallas_call(
        kern,
        grid=(m // tm, ne // te),
        in_specs=[
            pl.BlockSpec((tm, d), lambda i, j: (i, 0)),
            pl.BlockSpec((te, d), lambda i, j: (j, 0)),
            pl.BlockSpec((tm, nsel), lambda i, j: (i, 0)),
            pl.BlockSpec((tm, nsel), lambda i, j: (i, 0)),
        ],
        out_specs=pl.BlockSpec((tm, nsel), lambda i, j: (i, 0)),
        out_shape=jax.ShapeDtypeStruct((m, nsel), F32),
        scratch_shapes=[pltpu.VMEM((tm, DOT_COLS), F32)] * 2,
        compiler_params=_params("parallel", "arbitrary"),
        name="peer_asel",
    )(hn, u, e1, e2)


def _pout_kernel(h_ref, asel_ref, gate_ref, e1_ref, e2_ref, v_ref, og_ref, o_ref, w_scr, w3_scr,
                 g0_scr, g1_scr, *, tm, chunks, out_norm):
    grp = pl.program_id(1)

    @pl.when(grp == 0)
    def _():
        a = asel_ref[...]
        gelu = 0.5 * a * (1.0 + lax.erf(a * (2.0 ** -0.5)))
        w_scr[...] = gate_ref[...] * gelu
        o_ref[...] = h_ref[...]
        sub = lax.broadcasted_iota(jnp.int32, (N_KEYS, e1_ref.shape[1]), 0)

        def tiles(n, dst):
            t0 = pl.multiple_of(n * W_GROUP, W_GROUP)
            e1g = e1_ref[pl.ds(t0, W_GROUP), :]
            e2g = e2_ref[pl.ds(t0, W_GROUP), :]
            wg = w_scr[pl.ds(t0, W_GROUP), :]
            for t in range(W_GROUP):
                lhs = jnp.where(sub == e1g[t:t + 1], wg[t:t + 1], 0.0).astype(BF16)
                rhs = jnp.where(sub == e2g[t:t + 1], 1.0, 0.0).astype(BF16)
                dst[t * N_KEYS:(t + 1) * N_KEYS, :] = lax.dot_general(
                    lhs, rhs, _NT, preferred_element_type=F32)

        def regroup(n, src):
            t0 = pl.multiple_of(n * W_GROUP, W_GROUP)
            g = src[...].astype(BF16).reshape(W_GROUP, N_KEYS, N_KEYS)
            w3_scr[:, pl.ds(t0, W_GROUP), :] = jnp.swapaxes(g, 0, 1)

        groups = tm // W_GROUP
        assert groups % 2 == 0

        def body(m, carry):
            tiles(2 * m + 1, g1_scr)
            regroup(2 * m, g0_scr)
            tiles(2 * m + 2, g0_scr)
            regroup(2 * m + 1, g1_scr)
            return carry

        tiles(0, g0_scr)
        lax.fori_loop(0, groups // 2 - 1, body, 0)
        tiles(groups - 1, g1_scr)
        regroup(groups - 2, g0_scr)
        regroup(groups - 1, g1_scr)

    parts = [w3_scr[grp * chunks + c] for c in range(chunks)]
    o_ref[...] += _dot(jnp.concatenate(parts, axis=1), v_ref[...])

    if out_norm:
        @pl.when(grp == pl.num_programs(1) - 1)
        def _():
            o_ref[...] = _rms(o_ref[...], og_ref[...])


def _peer_out(h, asel, gate, e1, e2, v, out_g, *, tm, te, out_norm):
    m, d = h.shape
    ne = v.shape[0]
    nsel = e1.shape[1]
    kern = functools.partial(_pout_kernel, tm=tm, chunks=te // N_KEYS, out_norm=out_norm)
    sel = pl.BlockSpec((tm, nsel), lambda i, j: (i, 0))
    return pl.pallas_call(
        kern,
        grid=(m // tm, ne // te),
        in_specs=[
            pl.BlockSpec((tm, d), lambda i, j: (i, 0)),
            sel, sel, sel, sel,
            pl.BlockSpec((te, d), lambda i, j: (j, 0)),
            pl.BlockSpec((1, d), lambda i, j: (0, 0)),
        ],
        out_specs=pl.BlockSpec((tm, d), lambda i, j: (i, 0)),
        out_shape=jax.ShapeDtypeStruct((m, d), F32),
        scratch_shapes=[pltpu.VMEM((tm, nsel), F32), pltpu.VMEM((N_KEYS, tm, N_KEYS), BF16),
                        pltpu.VMEM((W_GROUP * N_KEYS, N_KEYS), F32),
                        pltpu.VMEM((W_GROUP * N_KEYS, N_KEYS), F32)],
        compiler_params=_params("parallel", "arbitrary"),
        name="peer_out",
    )(h, asel, gate, e1, e2, v, out_g.reshape(1, d))


def kernel(x, mem, norm_mix_g, w_in, conv_dw_w, conv_dw_b, conv_ln_g, conv_ln_b, sc_w, w_branch, b_gate, w_out, norm_xa_g, norm_mem_g, xa_wq, xa_wkv, xa_wo, norm_ffn_g, peer_wq, peer_keys, peer_u, peer_v, final_g):
    b, s, d = x.shape
    depth = w_in.shape[0]
    assert depth >= 1
    n_mem = mem.shape[1]
    sb_w = w_branch.shape[2]
    cv_w = conv_dw_w.shape[2]
    nb = w_branch.shape[1]
    qkv_w, conv_w = 3 * sb_w, 5 * cv_w
    m = b * s
    tm_big = min(1024, m)

    h = x.reshape(m, d)
    mem2 = mem.reshape(b * n_mem, d)
    for l in range(depth):
        w_in_l = w_in[l].astype(BF16)
        p_qkv = _norm_matmul(h, norm_mix_g[l], w_in_l, col0=0, ncols=qkv_w, out_dtype=BF16,
                             tm=tm_big, tn=512, name="in_proj_qkv")
        y_sb = _sb_attention(p_qkv.reshape(b, s, qkv_w), width=sb_w, bq=512, bk=256)
        y_cv, y_sc = _conv_branches(h.reshape(b, s, d), norm_mix_g[l], w_in_l, qkv_w, conv_dw_w[l],
                                    conv_dw_b[l], conv_ln_g[l], conv_ln_b[l], sc_w[l], ts=256)
        h = _merge(y_sb.reshape(m, sb_w), y_cv.reshape(m, cv_w), y_sc.reshape(m, cv_w), h,
                   norm_mix_g[l], w_in_l, qkv_w + conv_w, b_gate[l], w_branch[l].astype(BF16),
                   w_out[l].astype(BF16), tm=256)

        kv = _norm_matmul(mem2, norm_mem_g[l], xa_wkv[l].astype(BF16), col0=0, ncols=2 * d,
                          out_dtype=BF16, tm=min(1024, b * n_mem), tn=512, name="mem_kv")
        h = _cross_attention(h.reshape(b, s, d), norm_xa_g[l], xa_wq[l].astype(BF16),
                             kv.reshape(b, n_mem, 2 * d), xa_wo[l].astype(BF16), tm=256).reshape(m, d)

        keys = peer_keys[l].astype(BF16).reshape(PEER_HEADS * 2, N_KEYS, peer_keys.shape[-1])
        hn, e1, e2, gate = _peer_route(h, norm_ffn_g[l], peer_wq[l].astype(BF16), keys, tm=256)
        asel = _peer_asel(hn, peer_u[l].astype(BF16), e1, e2, tm=min(512, m), te=8192)
        h = _peer_out(h, asel, gate, e1, e2, peer_v[l].astype(BF16), final_g, tm=min(512, m),
                      te=4096, out_norm=(l == depth - 1))
    return h.reshape(b, s, d)
```

```python
import functools

import jax
import jax.numpy as jnp
from jax import lax
from jax.experimental import pallas as pl
from jax.experimental.pallas import tpu as pltpu

F32 = jnp.float32
BF16 = jnp.bfloat16

LANES = 128
SUBLANES = 8
MXU_WIDTH = 256
DOT_COLS = 2 * MXU_WIDTH
VMEM_LIMIT_BYTES = 56 * 1024 * 1024

SB_HEAD_DIM = 64
SB_HEADS_PER_TILE = LANES // SB_HEAD_DIM
XA_HEADS = 4
PEER_HEADS = 8
PEER_TOPK = 16
N_KEYS = 128
CONV_HALO = 32
ROUTE_HEADS_PER_TRIP = 8
W_PAIRS_PER_TRIP = 3
W_GROUP = 16

_NT = (((1,), (1,)), ((), ()))


def _params(*sem):
    return pltpu.CompilerParams(dimension_semantics=sem, vmem_limit_bytes=VMEM_LIMIT_BYTES)


def _rms(x, g, eps=1e-6):
    return x * lax.rsqrt(jnp.mean(x * x, axis=-1, keepdims=True) + eps) * g


def _dot(a, b):
    return jnp.dot(a, b, preferred_element_type=F32)


def _norm_matmul_kernel(x_ref, g_ref, w_ref, o_ref, xn_ref):
    @pl.when(pl.program_id(1) == 0)
    def _():
        xn_ref[...] = _rms(x_ref[...], g_ref[...]).astype(BF16)

    o_ref[...] = _dot(xn_ref[...], w_ref[...]).astype(o_ref.dtype)


def _norm_matmul(x, g, w, *, col0, ncols, out_dtype, tm, tn, name):
    m, k = x.shape
    off = col0 // tn
    assert col0 % tn == 0 and ncols % tn == 0 and m % tm == 0
    return pl.pallas_call(
        _norm_matmul_kernel,
        grid=(m // tm, ncols // tn),
        in_specs=[
            pl.BlockSpec((tm, k), lambda i, j: (i, 0)),
            pl.BlockSpec((1, k), lambda i, j: (0, 0)),
            pl.BlockSpec((k, tn), lambda i, j: (0, j + off)),
        ],
        out_specs=pl.BlockSpec((tm, tn), lambda i, j: (i, j)),
        out_shape=jax.ShapeDtypeStruct((m, ncols), out_dtype),
        scratch_shapes=[pltpu.VMEM((tm, k), BF16)],
        compiler_params=_params("parallel", "arbitrary"),
        name=name,
    )(x, g.reshape(1, k), w)


def _softplus(z):
    return jnp.maximum(z, 0.0) + jnp.log(1.0 + jnp.exp(-jnp.abs(z)))


def _sb_kernel(q_ref, k_ref, v_ref, o_ref, *, bq, bk, scale):
    i = pl.program_id(2)
    per_q = bq // bk
    q = q_ref[...]
    lane = lax.broadcasted_iota(jnp.int32, (bq, LANES), 1)
    trow = lax.broadcasted_iota(jnp.int32, (bk, bk), 0)
    tcol = lax.broadcasted_iota(jnp.int32, (bk, bk), 1)
    tri = jnp.where(trow > tcol, 1.0, 0.0).astype(BF16)
    tri2 = jnp.concatenate([tri, tri], axis=0)
    zero_q = jnp.zeros_like(q)
    heads = range(SB_HEADS_PER_TILE)
    qs = q * jnp.asarray(scale, q.dtype)
    qhs = [jnp.where((lane // SB_HEAD_DIM) == hh, qs, zero_q) for hh in heads]

    def step(carry, blocks):
        chains = [(h, n) for h in heads for n in range(len(blocks))]
        ks = [k_ref[pl.ds(pl.multiple_of(j * bk, bk), bk), :] for j, _, _ in blocks]
        vs = [v_ref[pl.ds(pl.multiple_of(j * bk, bk), bk), :] for j, _, _ in blocks]
        z = {(h, n): lax.dot_general(qhs[h][blocks[n][1]:], ks[n], _NT, preferred_element_type=F32)
             for h, n in chains}
        keep, lb, lk, split = {}, {}, {}, {}
        for ch in chains:
            sp = _softplus(z[ch])
            if blocks[ch[1]][2]:
                keep[ch] = (lax.broadcasted_iota(jnp.int32, sp.shape, 1)
                            < lax.broadcasted_iota(jnp.int32, sp.shape, 0))
                lk[ch] = jnp.where(keep[ch], -sp, 0.0)
            else:
                lk[ch] = -sp
            lb[ch] = z[ch] - sp
            hi = lk[ch].astype(BF16)
            lo = (lk[ch] - hi.astype(F32)).astype(BF16)
            split[ch] = jnp.concatenate([hi, lo], axis=1)
        after = {ch: _dot(split[ch], tri2) for ch in chains}
        mass = {ch: jnp.sum(lk[ch], axis=1, keepdims=True) for ch in chains}
        out = []
        for h in heads:
            acc, c = carry[h]
            for n, (_, r0, causal) in enumerate(blocks):
                a = jnp.exp(lb[h, n] + after[h, n] + c[r0:])
                if causal:
                    a = jnp.where(keep[h, n], a, 0.0)
                da, dc = _dot(a.astype(BF16), vs[n]), mass[h, n]
                if r0:
                    da = jnp.concatenate([jnp.zeros((r0, LANES), F32), da], axis=0)
                    dc = jnp.concatenate([jnp.zeros((r0, 1), F32), dc], axis=0)
                acc, c = acc + da, c + dc
            out.append((acc, c))
        return tuple(out)

    zero = (jnp.zeros((bq, LANES), F32), jnp.zeros((bq, 1), F32))
    carry = step(tuple(zero for _ in heads),
                 [(i * per_q + d, d * bk, True) for d in reversed(range(per_q))])
    n_left = i * per_q
    carry = lax.fori_loop(
        0, n_left // 2,
        lambda n, carry: step(
            carry, [(n_left - 1 - 2 * n, 0, False), (n_left - 2 - 2 * n, 0, False)]), carry)
    if per_q % 2:
        carry = lax.cond(n_left % 2 == 1, lambda carry: step(carry, [(0, 0, False)]),
                         lambda carry: carry, carry)
    o = carry[0][0]
    for hh in heads[1:]:
        o = jnp.where((lane // SB_HEAD_DIM) == hh, carry[hh][0], o)
    o_ref[...] = o.astype(o_ref.dtype)


def _sb_attention(p_qkv, *, width, bq, bk):
    b, s, _ = p_qkv.shape
    tiles = width // LANES
    bq, bk = min(bq, s), min(bk, s)
    assert bq % bk == 0 and s % bq == 0
    kern = functools.partial(_sb_kernel, bq=bq, bk=bk, scale=SB_HEAD_DIM ** -0.5)
    return pl.pallas_call(
        kern,
        grid=(b, tiles, s // bq),
        in_specs=[
            pl.BlockSpec((None, bq, LANES), lambda bi, t, i: (bi, i, t)),
            pl.BlockSpec((None, s, LANES), lambda bi, t, i: (bi, 0, tiles + t)),
            pl.BlockSpec((None, s, LANES), lambda bi, t, i: (bi, 0, 2 * tiles + t)),
        ],
        out_specs=pl.BlockSpec((None, bq, LANES), lambda bi, t, i: (bi, i, t)),
        out_shape=jax.ShapeDtypeStruct((b, s, width), BF16),
        compiler_params=_params("parallel", "parallel", "arbitrary"),
        name="sb_attention",
    )(p_qkv, p_qkv, p_qkv)


def _conv_kernel(h_ref, g_ref, w_val, w_gate, w_gb, w_gc, w_xin, dww_ref, dwb_ref, lng_ref, lnb_ref,
                 scw_ref, ycv_ref, ysc_ref, ubuf, sbuf, gb_scr, win_scr, *, ts, rows):
    t = pl.program_id(1)
    c = ubuf.shape[1]
    xn = _rms(h_ref[...], g_ref[...]).astype(BF16)

    @pl.when(t == 0)
    def _():
        ubuf[0:CONV_HALO, :] = jnp.zeros((CONV_HALO, c), F32)
        sbuf[0:CONV_HALO, :] = jnp.zeros((CONV_HALO, c), F32)

    @pl.when(t > 0)
    def _():
        ubuf[0:CONV_HALO, :] = ubuf[ts:ts + CONV_HALO, :]
        sbuf[0:CONV_HALO, :] = sbuf[ts:ts + CONV_HALO, :]

    ubuf[CONV_HALO:, :] = _dot(xn, w_val[...]) * jax.nn.sigmoid(_dot(xn, w_gate[...]))
    sbuf[CONV_HALO:, :] = _dot(xn, w_gc[...]) * _dot(xn, w_xin[...])
    gb_scr[...] = _dot(xn, w_gb[...])
    kw = dww_ref.shape[0]
    sw = scw_ref.shape[0]

    first = CONV_HALO - (kw - 1)
    for r0 in range(0, ts, rows):
        acc = jnp.zeros((rows, c), F32)
        for phase in range(SUBLANES):
            taps = [j for j in range(kw) if (first + j) % SUBLANES == phase]
            if not taps:
                continue
            lo, hi = first + taps[0], first + taps[-1]
            span = hi - lo + rows
            win_scr[phase, 0:span, :] = ubuf[r0 + lo:r0 + lo + span, :]
            for j in taps:
                off = first + j - lo
                acc = acc + dww_ref[j:j + 1, :] * win_scr[phase, off:off + rows, :]
        acc = acc + dwb_ref[...]
        mu = jnp.mean(acc, axis=-1, keepdims=True)
        d = acc - mu
        var = jnp.mean(d * d, axis=-1, keepdims=True)
        y = d * lax.rsqrt(var + 1e-5) * lng_ref[...] + lnb_ref[...]
        ycv_ref[r0:r0 + rows, :] = (y * jax.nn.sigmoid(y)).astype(ycv_ref.dtype)
        acc2 = jnp.zeros((rows, c), F32)
        for j in range(sw):
            lo = r0 + CONV_HALO - (sw - 1) + j
            acc2 = acc2 + scw_ref[j:j + 1, :] * sbuf[lo:lo + rows, :]
        ysc_ref[r0:r0 + rows, :] = (gb_scr[r0:r0 + rows, :] * acc2).astype(ysc_ref.dtype)


def _conv_branches(h, g, w_in, conv_col0, dww, dwb, lng, lnb, scw, *, ts):
    b, s, d = h.shape
    c = dww.shape[1]
    ts = min(ts, s)
    assert dww.shape[0] - 1 <= CONV_HALO and scw.shape[0] - 1 <= CONV_HALO and conv_col0 % c == 0
    rows = 64
    kern = functools.partial(_conv_kernel, ts=ts, rows=rows)
    w_col = lambda n: pl.BlockSpec((d, c), lambda bi, t, n=n: (0, conv_col0 // c + n))
    full = lambda a: pl.BlockSpec(a.shape, lambda bi, t: (0, 0))
    vecs = [dww, dwb.reshape(1, c), lng.reshape(1, c), lnb.reshape(1, c), scw]
    out_spec = pl.BlockSpec((None, ts, c), lambda bi, t: (bi, t, 0))
    return pl.pallas_call(
        kern,
        grid=(b, s // ts),
        in_specs=[pl.BlockSpec((None, ts, d), lambda bi, t: (bi, t, 0)),
                  pl.BlockSpec((1, d), lambda bi, t: (0, 0))]
        + [w_col(n) for n in range(5)] + [full(a) for a in vecs],
        out_specs=[out_spec, out_spec],
        out_shape=[jax.ShapeDtypeStruct((b, s, c), BF16)] * 2,
        scratch_shapes=[pltpu.VMEM((CONV_HALO + ts, c), F32)] * 2 + [
            pltpu.VMEM((ts, c), F32), pltpu.VMEM((SUBLANES, CONV_HALO + rows, c), F32)],
        compiler_params=_params("parallel", "arbitrary"),
        name="conv_branches",
    )(h, g.reshape(1, d), w_in, w_in, w_in, w_in, w_in, *vecs)


def _merge_kernel(ysb_ref, ycv_ref, ysc_ref, h_ref, g_ref, wg0_ref, wg1_ref, wg2_ref, bg_ref,
                  wb_ref, wo_ref, o_ref):
    h = h_ref[...]
    d = h.shape[1]
    xn = _rms(h, g_ref[...]).astype(BF16)
    merged = jnp.zeros(h.shape, F32)
    branches = ((ysb_ref, wg0_ref), (ycv_ref, wg1_ref), (ysc_ref, wg2_ref))
    for n, (y_ref, wg_ref) in enumerate(branches):
        proj = _dot(y_ref[...], wb_ref[n])
        gate = jax.nn.sigmoid(_dot(xn, wg_ref[...]) + bg_ref[:, n * d:(n + 1) * d])
        merged = merged + gate * proj
    o_ref[...] = h + _dot(merged.astype(BF16), wo_ref[...])


def _merge(ysb, ycv, ysc, h, g, w_in, gate_col0, b_gate, w_branch, w_out, *, tm):
    m, d = h.shape
    c = ysb.shape[1]
    nb = w_branch.shape[0]
    assert nb == 3 and gate_col0 % d == 0
    row = lambda w: pl.BlockSpec((tm, w), lambda i: (i, 0))
    gate_w = lambda n: pl.BlockSpec((d, d), lambda i, n=n: (0, gate_col0 // d + n))
    return pl.pallas_call(
        _merge_kernel,
        grid=(m // tm,),
        in_specs=[
            row(c), row(c), row(c), row(d),
            pl.BlockSpec((1, d), lambda i: (0, 0)),
            gate_w(0), gate_w(1), gate_w(2),
            pl.BlockSpec((1, nb * d), lambda i: (0, 0)),
            pl.BlockSpec((nb, c, d), lambda i: (0, 0, 0)),
            pl.BlockSpec((d, d), lambda i: (0, 0)),
        ],
        out_specs=row(d),
        out_shape=jax.ShapeDtypeStruct((m, d), F32),
        compiler_params=_params("parallel"),
        name="merge",
    )(ysb, ycv, ysc, h, g.reshape(1, d), w_in, w_in, w_in, b_gate.reshape(1, nb * d), w_branch, w_out)


def _xattn_kernel(h_ref, g_ref, wq_ref, kv_ref, wo_ref, o_ref):
    x = h_ref[...]
    d = x.shape[1]
    hd = d // XA_HEADS
    q = _dot(_rms(x, g_ref[...]).astype(BF16), wq_ref[...]).astype(BF16)
    outs = []
    for n in range(XA_HEADS):
        kh = kv_ref[:, n * hd:(n + 1) * hd]
        vh = kv_ref[:, d + n * hd:d + (n + 1) * hd]
        s = lax.dot_general(q[:, n * hd:(n + 1) * hd], kh, _NT, preferred_element_type=F32)
        s = s * (hd ** -0.5)
        e = jnp.exp(s - jnp.max(s, axis=-1, keepdims=True))
        a = e / jnp.sum(e, axis=-1, keepdims=True)
        outs.append(_dot(a.astype(BF16), vh).astype(BF16))
    o_ref[...] = x + _dot(jnp.concatenate(outs, axis=1), wo_ref[...])


def _cross_attention(h, g, wq, kv, wo, *, tm):
    b, s, d = h.shape
    n_mem = kv.shape[1]
    tm = min(tm, s)
    return pl.pallas_call(
        _xattn_kernel,
        grid=(b, s // tm),
        in_specs=[
            pl.BlockSpec((None, tm, d), lambda bi, i: (bi, i, 0)),
            pl.BlockSpec((1, d), lambda bi, i: (0, 0)),
            pl.BlockSpec((d, d), lambda bi, i: (0, 0)),
            pl.BlockSpec((None, n_mem, 2 * d), lambda bi, i: (bi, 0, 0)),
            pl.BlockSpec((d, d), lambda bi, i: (0, 0)),
        ],
        out_specs=pl.BlockSpec((None, tm, d), lambda bi, i: (bi, i, 0)),
        out_shape=jax.ShapeDtypeStruct((b, s, d), F32),
        compiler_params=_params("parallel", "parallel"),
        name="cross_attention",
    )(h, g.reshape(1, d), wq, kv, wo)


_KEY_POS = float(N_KEYS * N_KEYS)
_KEY_BIG = 2.0 ** 30

_SORT16 = (
    (0, 13), (1, 12), (2, 15), (3, 14), (4, 8), (5, 6), (7, 11), (9, 10),
    (0, 5), (1, 7), (2, 9), (3, 4), (6, 13), (8, 14), (10, 15), (11, 12),
    (0, 1), (2, 3), (4, 5), (6, 8), (7, 9), (10, 11), (12, 13), (14, 15),
    (0, 2), (1, 3), (4, 10), (5, 11), (6, 7), (8, 9), (12, 14), (13, 15),
    (1, 2), (3, 12), (4, 6), (5, 7), (8, 10), (9, 11), (13, 14),
    (1, 4), (2, 6), (5, 8), (7, 10), (9, 13), (11, 14),
    (2, 4), (3, 6), (9, 12), (11, 13),
    (3, 5), (6, 8), (7, 9), (10, 12),
    (3, 4), (5, 6), (7, 8), (9, 10), (11, 12),
    (6, 7), (8, 9),
)


def _sorted_topk(scores, k):
    depth = scores.shape[0] // SUBLANES
    assert depth == len({i for ce in _SORT16 for i in ce}) and k <= depth
    sub = lax.broadcasted_iota(jnp.int32, (SUBLANES, scores.shape[1]), 0).astype(F32)
    a = [scores[v * SUBLANES:(v + 1) * SUBLANES] for v in range(depth)]
    ix = [sub + float(v * SUBLANES) for v in range(depth)]
    for i, j in _SORT16:
        swap = (a[j] > a[i]) | ((a[j] == a[i]) & (ix[j] < ix[i]))
        a[i], a[j] = jnp.where(swap, a[j], a[i]), jnp.where(swap, a[i], a[j])
        ix[i], ix[j] = jnp.where(swap, ix[j], ix[i]), jnp.where(swap, ix[i], ix[j])
    vals, rows = [], []
    for n in range(k):
        m = jnp.max(a[0], axis=0, keepdims=True)
        row = jnp.min(jnp.where(a[0] == m, ix[0], float(scores.shape[0])), axis=0, keepdims=True)
        vals.append(m)
        rows.append(row)
        won = ix[0] == row
        for d in range(k - 1 - n):
            a[d] = jnp.where(won, a[d + 1], a[d])
            ix[d] = jnp.where(won, ix[d + 1], ix[d])
    return vals, rows


def _merge_pair_sums(v1, i1, v2, i2, k):
    irow = lax.broadcasted_iota(jnp.int32, v1.shape, 0)
    sums, keys = [], []
    for j in range(k):
        live = irow < k // (j + 1)
        sums.append(jnp.where(live, v1 + v2[j:j + 1], -jnp.inf))
        pos = (irow * k + j).astype(F32)
        keys.append(pos * _KEY_POS + i1 * float(N_KEYS) + i2[j:j + 1])
    tops, top_keys = [], []
    for n in range(k):
        m = jnp.max(sums[0], axis=0, keepdims=True)
        key = jnp.min(jnp.where(sums[0] == m, keys[0], _KEY_BIG), axis=0, keepdims=True)
        tops.append(m)
        top_keys.append(key)
        won = keys[0] == key
        for d in range(k - 1 - n):
            sums[d] = jnp.where(won, sums[d + 1], sums[d])
            keys[d] = jnp.where(won, keys[d + 1], keys[d])
    return tops, top_keys


def _route_kernel(h_ref, g_ref, wq_ref, keys_ref, hn_ref, e1_ref, e2_ref, gate_ref,
                  q_scr, v_scr, i_scr, top_scr, key_scr, e1t_scr, e2t_scr, gt_scr, *, tm):
    k = PEER_TOPK
    hn = _rms(h_ref[...], g_ref[...]).astype(BF16)
    hn_ref[...] = hn
    q_scr[...] = _dot(hn, wq_ref[...]).astype(BF16)

    def one_head(h, slot):
        for p in range(2):
            hp = h * 2 + p
            qs = q_scr[:, pl.ds(pl.multiple_of(hp * N_KEYS, N_KEYS), N_KEYS)]
            s = lax.dot_general(keys_ref[hp], qs, _NT, preferred_element_type=F32)
            for t0 in range(0, tm, LANES):
                vals, rows = _sorted_topk(s[:, t0:t0 + LANES], k)
                for n in range(k):
                    v_scr[slot, p, n:n + 1, t0:t0 + LANES] = vals[n]
                    i_scr[slot, p, n:n + 1, t0:t0 + LANES] = rows[n]
        for t0 in range(0, tm, LANES):
            lanes = slice(t0, t0 + LANES)
            tops, keys = _merge_pair_sums(v_scr[slot, 0, :, lanes], i_scr[slot, 0, :, lanes],
                                          v_scr[slot, 1, :, lanes], i_scr[slot, 1, :, lanes], k)
            for n in range(k):
                top_scr[slot, n:n + 1, lanes] = tops[n]
                key_scr[slot, n:n + 1, lanes] = keys[n]
        top = top_scr[slot]
        kk = key_scr[slot]
        e = kk - jnp.floor(kk * (1.0 / _KEY_POS)) * _KEY_POS
        e1 = jnp.floor(e * (1.0 / N_KEYS))
        e2 = e - e1 * float(N_KEYS)
        ex = jnp.exp(top - top[0:1])
        gate = ex / jnp.sum(ex, axis=0, keepdims=True)
        r0 = pl.multiple_of(h * k, k)
        e1t_scr[pl.ds(r0, k), :] = e1
        e2t_scr[pl.ds(r0, k), :] = e2
        gt_scr[pl.ds(r0, k), :] = gate

    def head_group(n, carry):
        for slot in range(ROUTE_HEADS_PER_TRIP):
            one_head(ROUTE_HEADS_PER_TRIP * n + slot, slot)
        return carry

    lax.fori_loop(0, PEER_HEADS // ROUTE_HEADS_PER_TRIP, head_group, 0)
    e1_ref[...] = e1t_scr[...].T.astype(jnp.int32)
    e2_ref[...] = e2t_scr[...].T.astype(jnp.int32)
    gate_ref[...] = gt_scr[...].T


def _peer_route(h, g, wq, keys, *, tm):
    m, d = h.shape
    qw = wq.shape[1]
    nsel = PEER_HEADS * PEER_TOPK
    k = PEER_TOPK
    kern = functools.partial(_route_kernel, tm=tm)
    row = lambda w: pl.BlockSpec((tm, w), lambda i: (i, 0))
    return pl.pallas_call(
        kern,
        grid=(m // tm,),
        in_specs=[
            row(d),
            pl.BlockSpec((1, d), lambda i: (0, 0)),
            pl.BlockSpec((d, qw), lambda i: (0, 0)),
            pl.BlockSpec(keys.shape, lambda i: (0, 0, 0)),
        ],
        out_specs=[row(d), row(nsel), row(nsel), row(nsel)],
        out_shape=[
            jax.ShapeDtypeStruct((m, d), BF16),
            jax.ShapeDtypeStruct((m, nsel), jnp.int32),
            jax.ShapeDtypeStruct((m, nsel), jnp.int32),
            jax.ShapeDtypeStruct((m, nsel), F32),
        ],
        scratch_shapes=[
            pltpu.VMEM((tm, qw), BF16),
            pltpu.VMEM((ROUTE_HEADS_PER_TRIP, 2, k, tm), F32),
            pltpu.VMEM((ROUTE_HEADS_PER_TRIP, 2, k, tm), F32),
            pltpu.VMEM((ROUTE_HEADS_PER_TRIP, k, tm), F32),
            pltpu.VMEM((ROUTE_HEADS_PER_TRIP, k, tm), F32),
            pltpu.VMEM((nsel, tm), F32),
            pltpu.VMEM((nsel, tm), F32),
            pltpu.VMEM((nsel, tm), F32),
        ],
        compiler_params=_params("parallel"),
        name="peer_route",
    )(h, g.reshape(1, d), wq, keys)


def _asel_kernel(hn_ref, u_ref, e1_ref, e2_ref, o_ref, a0_scr, a1_scr, *, chunks):
    grp = pl.program_id(1)

    @pl.when(grp == 0)
    def _():
        o_ref[...] = jnp.zeros(o_ref.shape, F32)

    hn = hn_ref[...]
    e1 = e1_ref[...]
    e2 = e2_ref[...]
    per_dot = DOT_COLS // N_KEYS
    nblk = chunks // per_dot
    assert nblk % 2 == 0

    def matmul(n, dst):
        rows = pl.ds(pl.multiple_of(n * DOT_COLS, DOT_COLS), DOT_COLS)
        dst[...] = lax.dot_general(hn, u_ref[rows, :], _NT, preferred_element_type=F32)

    def gather(n, src, sel):
        for c in range(per_dot):
            gathered = jnp.take_along_axis(src[:, c * N_KEYS:(c + 1) * N_KEYS], e2, axis=1)
            sel = jnp.where(e1 == grp * chunks + n * per_dot + c, gathered, sel)
        return sel

    def body(m, sel):
        matmul(2 * m + 1, a1_scr)
        sel = gather(2 * m, a0_scr, sel)
        matmul(2 * m + 2, a0_scr)
        return gather(2 * m + 1, a1_scr, sel)

    matmul(0, a0_scr)
    sel = lax.fori_loop(0, nblk // 2 - 1, body, o_ref[...])
    matmul(nblk - 1, a1_scr)
    sel = gather(nblk - 2, a0_scr, sel)
    o_ref[...] = gather(nblk - 1, a1_scr, sel)


def _peer_asel(hn, u, e1, e2, *, tm, te):
    m, d = hn.shape
    ne = u.shape[0]
    nsel = e1.shape[1]
    kern = functools.partial(_asel_kernel, chunks=te // N_KEYS)
    return pl.pallas_call(
        kern,
        grid=(m // tm, ne // te),
        in_specs=[
            pl.BlockSpec((tm, d), lambda i, j: (i, 0)),
            pl.BlockSpec((te, d), lambda i, j: (j, 0)),
            pl.BlockSpec((tm, nsel), lambda i, j: (i, 0)),
            pl.BlockSpec((tm, nsel), lambda i, j: (i, 0)),
        ],
        out_specs=pl.BlockSpec((tm, nsel), lambda i, j: (i, 0)),
        out_shape=jax.ShapeDtypeStruct((m, nsel), F32),
        scratch_shapes=[pltpu.VMEM((tm, DOT_COLS), F32)] * 2,
        compiler_params=_params("parallel", "arbitrary"),
        name="peer_asel",
    )(hn, u, e1, e2)


def _pout_kernel(h_ref, asel_ref, gate_ref, e1_ref, e2_ref, v_ref, og_ref, o_ref, w_scr, w3_scr,
                 g0_scr, g1_scr, *, tm, chunks, out_norm):
    grp = pl.program_id(1)

    @pl.when(grp == 0)
    def _():
        a = asel_ref[...]
        gelu = 0.5 * a * (1.0 + lax.erf(a * (2.0 ** -0.5)))
        w_scr[...] = gate_ref[...] * gelu
        o_ref[...] = h_ref[...]
        sub = lax.broadcasted_iota(jnp.int32, (N_KEYS, e1_ref.shape[1]), 0)

        def tiles(n, dst):
            t0 = pl.multiple_of(n * W_GROUP, W_GROUP)
            e1g = e1_ref[pl.ds(t0, W_GROUP), :]
            e2g = e2_ref[pl.ds(t0, W_GROUP), :]
            wg = w_scr[pl.ds(t0, W_GROUP), :]
            for t in range(W_GROUP):
                lhs = jnp.where(sub == e1g[t:t + 1], wg[t:t + 1], 0.0).astype(BF16)
                rhs = jnp.where(sub == e2g[t:t + 1], 1.0, 0.0).astype(BF16)
                dst[t * N_KEYS:(t + 1) * N_KEYS, :] = lax.dot_general(
                    lhs, rhs, _NT, preferred_element_type=F32)

        def regroup(n, src):
            t0 = pl.multiple_of(n * W_GROUP, W_GROUP)
            g = src[...].astype(BF16).reshape(W_GROUP, N_KEYS, N_KEYS)
            w3_scr[:, pl.ds(t0, W_GROUP), :] = jnp.swapaxes(g, 0, 1)

        groups = tm // W_GROUP
        assert groups % 2 == 0

        pairs = groups // 2 - 1
        assert pairs % W_PAIRS_PER_TRIP == 0

        def body(m, carry):
            for u in range(W_PAIRS_PER_TRIP):
                p = W_PAIRS_PER_TRIP * m + u
                tiles(2 * p + 1, g1_scr)
                regroup(2 * p, g0_scr)
                tiles(2 * p + 2, g0_scr)
                regroup(2 * p + 1, g1_scr)
            return carry

        tiles(0, g0_scr)
        lax.fori_loop(0, pairs // W_PAIRS_PER_TRIP, body, 0)
        tiles(groups - 1, g1_scr)
        regroup(groups - 2, g0_scr)
        regroup(groups - 1, g1_scr)

    parts = [w3_scr[grp * chunks + c] for c in range(chunks)]
    o_ref[...] += _dot(jnp.concatenate(parts, axis=1), v_ref[...])

    if out_norm:
        @pl.when(grp == pl.num_programs(1) - 1)
        def _():
            o_ref[...] = _rms(o_ref[...], og_ref[...])


def _peer_out(h, asel, gate, e1, e2, v, out_g, *, tm, te, out_norm):
    m, d = h.shape
    ne = v.shape[0]
    nsel = e1.shape[1]
    kern = functools.partial(_pout_kernel, tm=tm, chunks=te // N_KEYS, out_norm=out_norm)
    sel = pl.BlockSpec((tm, nsel), lambda i, j: (i, 0))
    return pl.pallas_call(
        kern,
        grid=(m // tm, ne // te),
        in_specs=[
            pl.BlockSpec((tm, d), lambda i, j: (i, 0)),
            sel, sel, sel, sel,
            pl.BlockSpec((te, d), lambda i, j: (j, 0)),
            pl.BlockSpec((1, d), lambda i, j: (0, 0)),
        ],
        out_specs=pl.BlockSpec((tm, d), lambda i, j: (i, 0)),
        out_shape=jax.ShapeDtypeStruct((m, d), F32),
        scratch_shapes=[pltpu.VMEM((tm, nsel), F32), pltpu.VMEM((N_KEYS, tm, N_KEYS), BF16),
                        pltpu.VMEM((W_GROUP * N_KEYS, N_KEYS), F32),
                        pltpu.VMEM((W_GROUP * N_KEYS, N_KEYS), F32)],
        compiler_params=_params("parallel", "arbitrary"),
        name="peer_out",
    )(h, asel, gate, e1, e2, v, out_g.reshape(1, d))


def kernel(x, mem, norm_mix_g, w_in, conv_dw_w, conv_dw_b, conv_ln_g, conv_ln_b, sc_w, w_branch, b_gate, w_out, norm_xa_g, norm_mem_g, xa_wq, xa_wkv, xa_wo, norm_ffn_g, peer_wq, peer_keys, peer_u, peer_v, final_g):
    b, s, d = x.shape
    depth = w_in.shape[0]
    assert depth >= 1
    n_mem = mem.shape[1]
    sb_w = w_branch.shape[2]
    cv_w = conv_dw_w.shape[2]
    nb = w_branch.shape[1]
    qkv_w, conv_w = 3 * sb_w, 5 * cv_w
    m = b * s
    tm_big = min(1024, m)

    h = x.reshape(m, d)
    mem2 = mem.reshape(b * n_mem, d)
    for l in range(depth):
        w_in_l = w_in[l].astype(BF16)
        p_qkv = _norm_matmul(h, norm_mix_g[l], w_in_l, col0=0, ncols=qkv_w, out_dtype=BF16,
                             tm=tm_big, tn=512, name="in_proj_qkv")
        y_sb = _sb_attention(p_qkv.reshape(b, s, qkv_w), width=sb_w, bq=512, bk=256)
        y_cv, y_sc = _conv_branches(h.reshape(b, s, d), norm_mix_g[l], w_in_l, qkv_w, conv_dw_w[l],
                                    conv_dw_b[l], conv_ln_g[l], conv_ln_b[l], sc_w[l], ts=256)
        h = _merge(y_sb.reshape(m, sb_w), y_cv.reshape(m, cv_w), y_sc.reshape(m, cv_w), h,
                   norm_mix_g[l], w_in_l, qkv_w + conv_w, b_gate[l], w_branch[l].astype(BF16),
                   w_out[l].astype(BF16), tm=256)

        kv = _norm_matmul(mem2, norm_mem_g[l], xa_wkv[l].astype(BF16), col0=0, ncols=2 * d,
                          out_dtype=BF16, tm=min(1024, b * n_mem), tn=512, name="mem_kv")
        h = _cross_attention(h.reshape(b, s, d), norm_xa_g[l], xa_wq[l].astype(BF16),
                             kv.reshape(b, n_mem, 2 * d), xa_wo[l].astype(BF16), tm=256).reshape(m, d)

        keys = peer_keys[l].astype(BF16).reshape(PEER_HEADS * 2, N_KEYS, peer_keys.shape[-1])
        hn, e1, e2, gate = _peer_route(h, norm_ffn_g[l], peer_wq[l].astype(BF16), keys, tm=256)
        asel = _peer_asel(hn, peer_u[l].astype(BF16), e1, e2, tm=min(512, m), te=8192)
        h = _peer_out(h, asel, gate, e1, e2, peer_v[l].astype(BF16), final_g, tm=min(512, m),
                      te=4096, out_norm=(l == depth - 1))
    return h.reshape(b, s, d)
```

```python
import functools

import jax
import jax.numpy as jnp
from jax import lax
from jax.experimental import pallas as pl
from jax.experimental.pallas import tpu as pltpu

F32 = jnp.float32
BF16 = jnp.bfloat16

LANES = 128
SUBLANES = 8
MXU_WIDTH = 256
DOT_COLS = 2 * MXU_WIDTH
VMEM_LIMIT_BYTES = 56 * 1024 * 1024

SB_HEAD_DIM = 64
SB_HEADS_PER_TILE = LANES // SB_HEAD_DIM
XA_HEADS = 4
PEER_HEADS = 8
PEER_TOPK = 16
N_KEYS = 128
CONV_HALO = 32
ROUTE_HEADS_PER_TRIP = 8
W_PAIRS_PER_TRIP = 3
W_GROUP = 16

_NT = (((1,), (1,)), ((), ()))


def _params(*sem):
    return pltpu.CompilerParams(dimension_semantics=sem, vmem_limit_bytes=VMEM_LIMIT_BYTES)


def _rms(x, g, eps=1e-6):
    return x * lax.rsqrt(jnp.mean(x * x, axis=-1, keepdims=True) + eps) * g


def _dot(a, b):
    return jnp.dot(a, b, preferred_element_type=F32)


def _norm_matmul_kernel(x_ref, g_ref, w_ref, o_ref, xn_ref):
    @pl.when(pl.program_id(1) == 0)
    def _():
        xn_ref[...] = _rms(x_ref[...], g_ref[...]).astype(BF16)

    o_ref[...] = _dot(xn_ref[...], w_ref[...]).astype(o_ref.dtype)


def _norm_matmul(x, g, w, *, col0, ncols, out_dtype, tm, tn, name):
    m, k = x.shape
    off = col0 // tn
    assert col0 % tn == 0 and ncols % tn == 0 and m % tm == 0
    return pl.pallas_call(
        _norm_matmul_kernel,
        grid=(m // tm, ncols // tn),
        in_specs=[
            pl.BlockSpec((tm, k), lambda i, j: (i, 0)),
            pl.BlockSpec((1, k), lambda i, j: (0, 0)),
            pl.BlockSpec((k, tn), lambda i, j: (0, j + off)),
        ],
        out_specs=pl.BlockSpec((tm, tn), lambda i, j: (i, j)),
        out_shape=jax.ShapeDtypeStruct((m, ncols), out_dtype),
        scratch_shapes=[pltpu.VMEM((tm, k), BF16)],
        compiler_params=_params("parallel", "arbitrary"),
        name=name,
    )(x, g.reshape(1, k), w)


def _softplus(z):
    return jnp.maximum(z, 0.0) + jnp.log(1.0 + jnp.exp(-jnp.abs(z)))


def _sb_kernel(q_ref, k_ref, v_ref, o_ref, *, bq, bk, scale):
    i = pl.program_id(2)
    per_q = bq // bk
    q = q_ref[...]
    lane = lax.broadcasted_iota(jnp.int32, (bq, LANES), 1)
    trow = lax.broadcasted_iota(jnp.int32, (bk, bk), 0)
    tcol = lax.broadcasted_iota(jnp.int32, (bk, bk), 1)
    tri = jnp.where(trow > tcol, 1.0, 0.0).astype(BF16)
    tri2 = jnp.concatenate([tri, tri], axis=0)
    zero_q = jnp.zeros_like(q)
    heads = range(SB_HEADS_PER_TILE)
    qs = q * jnp.asarray(scale, q.dtype)
    qhs = [jnp.where((lane // SB_HEAD_DIM) == hh, qs, zero_q) for hh in heads]

    def step(carry, blocks):
        chains = [(h, n) for h in heads for n in range(len(blocks))]
        ks = [k_ref[pl.ds(pl.multiple_of(j * bk, bk), bk), :] for j, _, _ in blocks]
        vs = [v_ref[pl.ds(pl.multiple_of(j * bk, bk), bk), :] for j, _, _ in blocks]
        z = {(h, n): lax.dot_general(qhs[h][blocks[n][1]:], ks[n], _NT, preferred_element_type=F32)
             for h, n in chains}
        keep, lb, lk, split = {}, {}, {}, {}
        for ch in chains:
            sp = _softplus(z[ch])
            if blocks[ch[1]][2]:
                keep[ch] = (lax.broadcasted_iota(jnp.int32, sp.shape, 1)
                            < lax.broadcasted_iota(jnp.int32, sp.shape, 0))
                lk[ch] = jnp.where(keep[ch], -sp, 0.0)
            else:
                lk[ch] = -sp
            lb[ch] = z[ch] - sp
            hi = lk[ch].astype(BF16)
            lo = (lk[ch] - hi.astype(F32)).astype(BF16)
            split[ch] = jnp.concatenate([hi, lo], axis=1)
        after = {ch: _dot(split[ch], tri2) for ch in chains}
        mass = {ch: jnp.sum(lk[ch], axis=1, keepdims=True) for ch in chains}
        out = []
        for h in heads:
            acc, c = carry[h]
            for n, (_, r0, causal) in enumerate(blocks):
                a = jnp.exp(lb[h, n] + after[h, n] + c[r0:])
                if causal:
                    a = jnp.where(keep[h, n], a, 0.0)
                da, dc = _dot(a.astype(BF16), vs[n]), mass[h, n]
                if r0:
                    da = jnp.concatenate([jnp.zeros((r0, LANES), F32), da], axis=0)
                    dc = jnp.concatenate([jnp.zeros((r0, 1), F32), dc], axis=0)
                acc, c = acc + da, c + dc
            out.append((acc, c))
        return tuple(out)

    zero = (jnp.zeros((bq, LANES), F32), jnp.zeros((bq, 1), F32))
    carry = step(tuple(zero for _ in heads),
                 [(i * per_q + d, d * bk, True) for d in reversed(range(per_q))])
    n_left = i * per_q
    carry = lax.fori_loop(
        0, n_left // 2,
        lambda n, carry: step(
            carry, [(n_left - 1 - 2 * n, 0, False), (n_left - 2 - 2 * n, 0, False)]), carry)
    if per_q % 2:
        carry = lax.cond(n_left % 2 == 1, lambda carry: step(carry, [(0, 0, False)]),
                         lambda carry: carry, carry)
    o = carry[0][0]
    for hh in heads[1:]:
        o = jnp.where((lane // SB_HEAD_DIM) == hh, carry[hh][0], o)
    o_ref[...] = o.astype(o_ref.dtype)


def _sb_attention(p_qkv, *, width, bq, bk):
    b, s, _ = p_qkv.shape
    tiles = width // LANES
    bq, bk = min(bq, s), min(bk, s)
    assert bq % bk == 0 and s % bq == 0
    kern = functools.partial(_sb_kernel, bq=bq, bk=bk, scale=SB_HEAD_DIM ** -0.5)
    return pl.pallas_call(
        kern,
        grid=(b, tiles, s // bq),
        in_specs=[
            pl.BlockSpec((None, bq, LANES), lambda bi, t, i: (bi, i, t)),
            pl.BlockSpec((None, s, LANES), lambda bi, t, i: (bi, 0, tiles + t)),
            pl.BlockSpec((None, s, LANES), lambda bi, t, i: (bi, 0, 2 * tiles + t)),
        ],
        out_specs=pl.BlockSpec((None, bq, LANES), lambda bi, t, i: (bi, i, t)),
        out_shape=jax.ShapeDtypeStruct((b, s, width), BF16),
        compiler_params=_params("parallel", "parallel", "arbitrary"),
        name="sb_attention",
    )(p_qkv, p_qkv, p_qkv)


def _conv_kernel(h_ref, g_ref, w_val, w_gate, w_gb, w_gc, w_xin, dww_ref, dwb_ref, lng_ref, lnb_ref,
                 scw_ref, ycv_ref, ysc_ref, ubuf, sbuf, gb_scr, win_scr, *, ts, rows):
    t = pl.program_id(1)
    c = ubuf.shape[1]
    xn = _rms(h_ref[...], g_ref[...]).astype(BF16)

    @pl.when(t == 0)
    def _():
        ubuf[0:CONV_HALO, :] = jnp.zeros((CONV_HALO, c), F32)
        sbuf[0:CONV_HALO, :] = jnp.zeros((CONV_HALO, c), F32)

    @pl.when(t > 0)
    def _():
        ubuf[0:CONV_HALO, :] = ubuf[ts:ts + CONV_HALO, :]
        sbuf[0:CONV_HALO, :] = sbuf[ts:ts + CONV_HALO, :]

    ubuf[CONV_HALO:, :] = _dot(xn, w_val[...]) * jax.nn.sigmoid(_dot(xn, w_gate[...]))
    sbuf[CONV_HALO:, :] = _dot(xn, w_gc[...]) * _dot(xn, w_xin[...])
    gb_scr[...] = _dot(xn, w_gb[...])
    kw = dww_ref.shape[0]
    sw = scw_ref.shape[0]

    first = CONV_HALO - (kw - 1)
    for r0 in range(0, ts, rows):
        acc = jnp.zeros((rows, c), F32)
        for phase in range(SUBLANES):
            taps = [j for j in range(kw) if (first + j) % SUBLANES == phase]
            if not taps:
                continue
            lo, hi = first + taps[0], first + taps[-1]
            span = hi - lo + rows
            win_scr[phase, 0:span, :] = ubuf[r0 + lo:r0 + lo + span, :]
            for j in taps:
                off = first + j - lo
                acc = acc + dww_ref[j:j + 1, :] * win_scr[phase, off:off + rows, :]
        acc = acc + dwb_ref[...]
        mu = jnp.mean(acc, axis=-1, keepdims=True)
        d = acc - mu
        var = jnp.mean(d * d, axis=-1, keepdims=True)
        y = d * lax.rsqrt(var + 1e-5) * lng_ref[...] + lnb_ref[...]
        ycv_ref[r0:r0 + rows, :] = (y * jax.nn.sigmoid(y)).astype(ycv_ref.dtype)
        acc2 = jnp.zeros((rows, c), F32)
        for j in range(sw):
            lo = r0 + CONV_HALO - (sw - 1) + j
            acc2 = acc2 + scw_ref[j:j + 1, :] * sbuf[lo:lo + rows, :]
        ysc_ref[r0:r0 + rows, :] = (gb_scr[r0:r0 + rows, :] * acc2).astype(ysc_ref.dtype)


def _conv_branches(h, g, w_in, conv_col0, dww, dwb, lng, lnb, scw, *, ts):
    b, s, d = h.shape
    c = dww.shape[1]
    ts = min(ts, s)
    assert dww.shape[0] - 1 <= CONV_HALO and scw.shape[0] - 1 <= CONV_HALO and conv_col0 % c == 0
    rows = 64
    kern = functools.partial(_conv_kernel, ts=ts, rows=rows)
    w_col = lambda n: pl.BlockSpec((d, c), lambda bi, t, n=n: (0, conv_col0 // c + n))
    full = lambda a: pl.BlockSpec(a.shape, lambda bi, t: (0, 0))
    vecs = [dww, dwb.reshape(1, c), lng.reshape(1, c), lnb.reshape(1, c), scw]
    out_spec = pl.BlockSpec((None, ts, c), lambda bi, t: (bi, t, 0))
    return pl.pallas_call(
        kern,
        grid=(b, s // ts),
        in_specs=[pl.BlockSpec((None, ts, d), lambda bi, t: (bi, t, 0)),
                  pl.BlockSpec((1, d), lambda bi, t: (0, 0))]
        + [w_col(n) for n in range(5)] + [full(a) for a in vecs],
        out_specs=[out_spec, out_spec],
        out_shape=[jax.ShapeDtypeStruct((b, s, c), BF16)] * 2,
        scratch_shapes=[pltpu.VMEM((CONV_HALO + ts, c), F32)] * 2 + [
            pltpu.VMEM((ts, c), F32), pltpu.VMEM((SUBLANES, CONV_HALO + rows, c), F32)],
        compiler_params=_params("parallel", "arbitrary"),
        name="conv_branches",
    )(h, g.reshape(1, d), w_in, w_in, w_in, w_in, w_in, *vecs)


def _merge_kernel(ysb_ref, ycv_ref, ysc_ref, h_ref, g_ref, wg0_ref, wg1_ref, wg2_ref, bg_ref,
                  wb_ref, wo_ref, o_ref):
    h = h_ref[...]
    d = h.shape[1]
    xn = _rms(h, g_ref[...]).astype(BF16)
    merged = jnp.zeros(h.shape, F32)
    branches = ((ysb_ref, wg0_ref), (ycv_ref, wg1_ref), (ysc_ref, wg2_ref))
    for n, (y_ref, wg_ref) in enumerate(branches):
        proj = _dot(y_ref[...], wb_ref[n])
        gate = jax.nn.sigmoid(_dot(xn, wg_ref[...]) + bg_ref[:, n * d:(n + 1) * d])
        merged = merged + gate * proj
    o_ref[...] = h + _dot(merged.astype(BF16), wo_ref[...])


def _merge(ysb, ycv, ysc, h, g, w_in, gate_col0, b_gate, w_branch, w_out, *, tm):
    m, d = h.shape
    c = ysb.shape[1]
    nb = w_branch.shape[0]
    assert nb == 3 and gate_col0 % d == 0
    row = lambda w: pl.BlockSpec((tm, w), lambda i: (i, 0))
    gate_w = lambda n: pl.BlockSpec((d, d), lambda i, n=n: (0, gate_col0 // d + n))
    return pl.pallas_call(
        _merge_kernel,
        grid=(m // tm,),
        in_specs=[
            row(c), row(c), row(c), row(d),
            pl.BlockSpec((1, d), lambda i: (0, 0)),
            gate_w(0), gate_w(1), gate_w(2),
            pl.BlockSpec((1, nb * d), lambda i: (0, 0)),
            pl.BlockSpec((nb, c, d), lambda i: (0, 0, 0)),
            pl.BlockSpec((d, d), lambda i: (0, 0)),
        ],
        out_specs=row(d),
        out_shape=jax.ShapeDtypeStruct((m, d), F32),
        compiler_params=_params("parallel"),
        name="merge",
    )(ysb, ycv, ysc, h, g.reshape(1, d), w_in, w_in, w_in, b_gate.reshape(1, nb * d), w_branch, w_out)


def _xattn_kernel(h_ref, g_ref, wq_ref, kv_ref, wo_ref, o_ref):
    x = h_ref[...]
    d = x.shape[1]
    hd = d // XA_HEADS
    q = _dot(_rms(x, g_ref[...]).astype(BF16), wq_ref[...]).astype(BF16)
    outs = []
    for n in range(XA_HEADS):
        kh = kv_ref[:, n * hd:(n + 1) * hd]
        vh = kv_ref[:, d + n * hd:d + (n + 1) * hd]
        s = lax.dot_general(q[:, n * hd:(n + 1) * hd], kh, _NT, preferred_element_type=F32)
        s = s * (hd ** -0.5)
        e = jnp.exp(s - jnp.max(s, axis=-1, keepdims=True))
        a = e / jnp.sum(e, axis=-1, keepdims=True)
        outs.append(_dot(a.astype(BF16), vh).astype(BF16))
    o_ref[...] = x + _dot(jnp.concatenate(outs, axis=1), wo_ref[...])


def _cross_attention(h, g, wq, kv, wo, *, tm):
    b, s, d = h.shape
    n_mem = kv.shape[1]
    tm = min(tm, s)
    return pl.pallas_call(
        _xattn_kernel,
        grid=(b, s // tm),
        in_specs=[
            pl.BlockSpec((None, tm, d), lambda bi, i: (bi, i, 0)),
            pl.BlockSpec((1, d), lambda bi, i: (0, 0)),
            pl.BlockSpec((d, d), lambda bi, i: (0, 0)),
            pl.BlockSpec((None, n_mem, 2 * d), lambda bi, i: (bi, 0, 0)),
            pl.BlockSpec((d, d), lambda bi, i: (0, 0)),
        ],
        out_specs=pl.BlockSpec((None, tm, d), lambda bi, i: (bi, i, 0)),
        out_shape=jax.ShapeDtypeStruct((b, s, d), F32),
        compiler_params=_params("parallel", "parallel"),
        name="cross_attention",
    )(h, g.reshape(1, d), wq, kv, wo)


_KEY_POS = float(N_KEYS * N_KEYS)
_KEY_BIG = 2.0 ** 30

_SORT16 = (
    (0, 13), (1, 12), (2, 15), (3, 14), (4, 8), (5, 6), (7, 11), (9, 10),
    (0, 5), (1, 7), (2, 9), (3, 4), (6, 13), (8, 14), (10, 15), (11, 12),
    (0, 1), (2, 3), (4, 5), (6, 8), (7, 9), (10, 11), (12, 13), (14, 15),
    (0, 2), (1, 3), (4, 10), (5, 11), (6, 7), (8, 9), (12, 14), (13, 15),
    (1, 2), (3, 12), (4, 6), (5, 7), (8, 10), (9, 11), (13, 14),
    (1, 4), (2, 6), (5, 8), (7, 10), (9, 13), (11, 14),
    (2, 4), (3, 6), (9, 12), (11, 13),
    (3, 5), (6, 8), (7, 9), (10, 12),
    (3, 4), (5, 6), (7, 8), (9, 10), (11, 12),
    (6, 7), (8, 9),
)


def _sorted_topk(scores, k):
    depth = scores.shape[0] // SUBLANES
    assert depth == len({i for ce in _SORT16 for i in ce}) and k <= depth
    sub = lax.broadcasted_iota(jnp.int32, (SUBLANES, scores.shape[1]), 0).astype(F32)
    a = [scores[v * SUBLANES:(v + 1) * SUBLANES] for v in range(depth)]
    ix = [sub + float(v * SUBLANES) for v in range(depth)]
    for i, j in _SORT16:
        swap = (a[j] > a[i]) | ((a[j] == a[i]) & (ix[j] < ix[i]))
        a[i], a[j] = jnp.where(swap, a[j], a[i]), jnp.where(swap, a[i], a[j])
        ix[i], ix[j] = jnp.where(swap, ix[j], ix[i]), jnp.where(swap, ix[i], ix[j])
    vals, rows = [], []
    for n in range(k):
        m = jnp.max(a[0], axis=0, keepdims=True)
        row = jnp.min(jnp.where(a[0] == m, ix[0], float(scores.shape[0])), axis=0, keepdims=True)
        vals.append(m)
        rows.append(row)
        won = ix[0] == row
        for d in range(k - 1 - n):
            a[d] = jnp.where(won, a[d + 1], a[d])
            ix[d] = jnp.where(won, ix[d + 1], ix[d])
    return vals, rows


def _merge_pair_sums(v1, i1, v2, i2, k):
    irow = lax.broadcasted_iota(jnp.int32, v1.shape, 0)
    sums, keys = [], []
    for j in range(k):
        live = irow < k // (j + 1)
        sums.append(jnp.where(live, v1 + v2[j:j + 1], -jnp.inf))
        pos = (irow * k + j).astype(F32)
        keys.append(pos * _KEY_POS + i1 * float(N_KEYS) + i2[j:j + 1])
    tops, top_keys = [], []
    for n in range(k):
        m = jnp.max(sums[0], axis=0, keepdims=True)
        key = jnp.min(jnp.where(sums[0] == m, keys[0], _KEY_BIG), axis=0, keepdims=True)
        tops.append(m)
        top_keys.append(key)
        won = keys[0] == key
        for d in range(k - 1 - n):
            sums[d] = jnp.where(won, sums[d + 1], sums[d])
            keys[d] = jnp.where(won, keys[d + 1], keys[d])
    return tops, top_keys


def _route_kernel(h_ref, g_ref, wq_ref, keys_ref, hn_ref, e1_ref, e2_ref, gate_ref,
                  q_scr, v_scr, i_scr, top_scr, key_scr, e1t_scr, e2t_scr, gt_scr, *, tm):
    k = PEER_TOPK
    hn = _rms(h_ref[...], g_ref[...]).astype(BF16)
    hn_ref[...] = hn
    q_scr[...] = _dot(hn, wq_ref[...]).astype(BF16)

    def one_head(h, slot):
        for p in range(2):
            hp = h * 2 + p
            qs = q_scr[:, pl.ds(pl.multiple_of(hp * N_KEYS, N_KEYS), N_KEYS)]
            s = lax.dot_general(keys_ref[hp], qs, _NT, preferred_element_type=F32)
            for t0 in range(0, tm, LANES):
                vals, rows = _sorted_topk(s[:, t0:t0 + LANES], k)
                for n in range(k):
                    v_scr[slot, p, n:n + 1, t0:t0 + LANES] = vals[n]
                    i_scr[slot, p, n:n + 1, t0:t0 + LANES] = rows[n]
        for t0 in range(0, tm, LANES):
            lanes = slice(t0, t0 + LANES)
            tops, keys = _merge_pair_sums(v_scr[slot, 0, :, lanes], i_scr[slot, 0, :, lanes],
                                          v_scr[slot, 1, :, lanes], i_scr[slot, 1, :, lanes], k)
            for n in range(k):
                top_scr[slot, n:n + 1, lanes] = tops[n]
                key_scr[slot, n:n + 1, lanes] = keys[n]
        top = top_scr[slot]
        kk = key_scr[slot]
        e = kk - jnp.floor(kk * (1.0 / _KEY_POS)) * _KEY_POS
        e1 = jnp.floor(e * (1.0 / N_KEYS))
        e2 = e - e1 * float(N_KEYS)
        ex = jnp.exp(top - top[0:1])
        gate = ex / jnp.sum(ex, axis=0, keepdims=True)
        r0 = pl.multiple_of(h * k, k)
        e1t_scr[pl.ds(r0, k), :] = e1
        e2t_scr[pl.ds(r0, k), :] = e2
        gt_scr[pl.ds(r0, k), :] = gate

    def head_group(n, carry):
        for slot in range(ROUTE_HEADS_PER_TRIP):
            one_head(ROUTE_HEADS_PER_TRIP * n + slot, slot)
        return carry

    lax.fori_loop(0, PEER_HEADS // ROUTE_HEADS_PER_TRIP, head_group, 0)
    e1_ref[...] = e1t_scr[...].T.astype(jnp.int32)
    e2_ref[...] = e2t_scr[...].T.astype(jnp.int32)
    gate_ref[...] = gt_scr[...].T


def _peer_route(h, g, wq, keys, *, tm):
    m, d = h.shape
    qw = wq.shape[1]
    nsel = PEER_HEADS * PEER_TOPK
    k = PEER_TOPK
    kern = functools.partial(_route_kernel, tm=tm)
    row = lambda w: pl.BlockSpec((tm, w), lambda i: (i, 0))
    return pl.pallas_call(
        kern,
        grid=(m // tm,),
        in_specs=[
            row(d),
            pl.BlockSpec((1, d), lambda i: (0, 0)),
            pl.BlockSpec((d, qw), lambda i: (0, 0)),
            pl.BlockSpec(keys.shape, lambda i: (0, 0, 0)),
        ],
        out_specs=[row(d), row(nsel), row(nsel), row(nsel)],
        out_shape=[
            jax.ShapeDtypeStruct((m, d), BF16),
            jax.ShapeDtypeStruct((m, nsel), jnp.int32),
            jax.ShapeDtypeStruct((m, nsel), jnp.int32),
            jax.ShapeDtypeStruct((m, nsel), F32),
        ],
        scratch_shapes=[
            pltpu.VMEM((tm, qw), BF16),
            pltpu.VMEM((ROUTE_HEADS_PER_TRIP, 2, k, tm), F32),
            pltpu.VMEM((ROUTE_HEADS_PER_TRIP, 2, k, tm), F32),
            pltpu.VMEM((ROUTE_HEADS_PER_TRIP, k, tm), F32),
            pltpu.VMEM((ROUTE_HEADS_PER_TRIP, k, tm), F32),
            pltpu.VMEM((nsel, tm), F32),
            pltpu.VMEM((nsel, tm), F32),
            pltpu.VMEM((nsel, tm), F32),
        ],
        compiler_params=_params("parallel"),
        name="peer_route",
    )(h, g.reshape(1, d), wq, keys)


def _asel_kernel(hn_ref, u_ref, e1_ref, e2_ref, o_ref, a0_scr, a1_scr, *, chunks):
    grp = pl.program_id(1)

    @pl.when(grp == 0)
    def _():
        o_ref[...] = jnp.zeros(o_ref.shape, F32)

    hn = hn_ref[...]
    e1 = e1_ref[...]
    e2 = e2_ref[...]
    per_dot = DOT_COLS // N_KEYS
    nblk = chunks // per_dot
    assert nblk % 2 == 0

    def matmul(n, dst):
        rows = pl.ds(pl.multiple_of(n * DOT_COLS, DOT_COLS), DOT_COLS)
        dst[...] = lax.dot_general(hn, u_ref[rows, :], _NT, preferred_element_type=F32)

    def gather(n, src, sel):
        for c in range(per_dot):
            gathered = jnp.take_along_axis(src[:, c * N_KEYS:(c + 1) * N_KEYS], e2, axis=1)
            sel = jnp.where(e1 == grp * chunks + n * per_dot + c, gathered, sel)
        return sel

    def body(m, sel):
        matmul(2 * m + 1, a1_scr)
        sel = gather(2 * m, a0_scr, sel)
        matmul(2 * m + 2, a0_scr)
        return gather(2 * m + 1, a1_scr, sel)

    matmul(0, a0_scr)
    sel = lax.fori_loop(0, nblk // 2 - 1, body, o_ref[...])
    matmul(nblk - 1, a1_scr)
    sel = gather(nblk - 2, a0_scr, sel)
    o_ref[...] = gather(nblk - 1, a1_scr, sel)


def _peer_asel(hn, u, e1, e2, *, tm, te):
    m, d = hn.shape
    ne = u.shape[0]
    nsel = e1.shape[1]
    kern = functools.partial(_asel_kernel, chunks=te // N_KEYS)
    table_mode = pl.Buffered(1) if te == ne else None
    return pl.pallas_call(
        kern,
        grid=(m // tm, ne // te),
        in_specs=[
            pl.BlockSpec((tm, d), lambda i, j: (i, 0)),
            pl.BlockSpec((te, d), lambda i, j: (j, 0), pipeline_mode=table_mode),
            pl.BlockSpec((tm, nsel), lambda i, j: (i, 0)),
            pl.BlockSpec((tm, nsel), lambda i, j: (i, 0)),
        ],
        out_specs=pl.BlockSpec((tm, nsel), lambda i, j: (i, 0)),
        out_shape=jax.ShapeDtypeStruct((m, nsel), F32),
        scratch_shapes=[pltpu.VMEM((tm, DOT_COLS), F32)] * 2,
        compiler_params=_params("parallel", "arbitrary"),
        name="peer_asel",
    )(hn, u, e1, e2)


def _pout_kernel(h_ref, asel_ref, gate_ref, e1_ref, e2_ref, v_ref, og_ref, o_ref, w_scr, w3_scr,
                 g0_scr, g1_scr, *, tm, chunks, out_norm):
    grp = pl.program_id(1)

    @pl.when(grp == 0)
    def _():
        a = asel_ref[...]
        gelu = 0.5 * a * (1.0 + lax.erf(a * (2.0 ** -0.5)))
        w_scr[...] = gate_ref[...] * gelu
        o_ref[...] = h_ref[...]
        sub = lax.broadcasted_iota(jnp.int32, (N_KEYS, e1_ref.shape[1]), 0)

        def tiles(n, dst):
            t0 = pl.multiple_of(n * W_GROUP, W_GROUP)
            e1g = e1_ref[pl.ds(t0, W_GROUP), :]
            e2g = e2_ref[pl.ds(t0, W_GROUP), :]
            wg = w_scr[pl.ds(t0, W_GROUP), :]
            for t in range(W_GROUP):
                lhs = jnp.where(sub == e1g[t:t + 1], wg[t:t + 1], 0.0).astype(BF16)
                rhs = jnp.where(sub == e2g[t:t + 1], 1.0, 0.0).astype(BF16)
                dst[t * N_KEYS:(t + 1) * N_KEYS, :] = lax.dot_general(
                    lhs, rhs, _NT, preferred_element_type=F32)

        def regroup(n, src):
            t0 = pl.multiple_of(n * W_GROUP, W_GROUP)
            g = src[...].astype(BF16).reshape(W_GROUP, N_KEYS, N_KEYS)
            w3_scr[:, pl.ds(t0, W_GROUP), :] = jnp.swapaxes(g, 0, 1)

        groups = tm // W_GROUP
        assert groups % 2 == 0

        pairs = groups // 2 - 1
        assert pairs % W_PAIRS_PER_TRIP == 0

        def body(m, carry):
            for u in range(W_PAIRS_PER_TRIP):
                p = W_PAIRS_PER_TRIP * m + u
                tiles(2 * p + 1, g1_scr)
                regroup(2 * p, g0_scr)
                tiles(2 * p + 2, g0_scr)
                regroup(2 * p + 1, g1_scr)
            return carry

        tiles(0, g0_scr)
        lax.fori_loop(0, pairs // W_PAIRS_PER_TRIP, body, 0)
        tiles(groups - 1, g1_scr)
        regroup(groups - 2, g0_scr)
        regroup(groups - 1, g1_scr)

    parts = [w3_scr[grp * chunks + c] for c in range(chunks)]
    o_ref[...] += _dot(jnp.concatenate(parts, axis=1), v_ref[...])

    if out_norm:
        @pl.when(grp == pl.num_programs(1) - 1)
        def _():
            o_ref[...] = _rms(o_ref[...], og_ref[...])


def _peer_out(h, asel, gate, e1, e2, v, out_g, *, tm, te, out_norm):
    m, d = h.shape
    ne = v.shape[0]
    nsel = e1.shape[1]
    kern = functools.partial(_pout_kernel, tm=tm, chunks=te // N_KEYS, out_norm=out_norm)
    sel = pl.BlockSpec((tm, nsel), lambda i, j: (i, 0))
    return pl.pallas_call(
        kern,
        grid=(m // tm, ne // te),
        in_specs=[
            pl.BlockSpec((tm, d), lambda i, j: (i, 0)),
            sel, sel, sel, sel,
            pl.BlockSpec((te, d), lambda i, j: (j, 0)),
            pl.BlockSpec((1, d), lambda i, j: (0, 0)),
        ],
        out_specs=pl.BlockSpec((tm, d), lambda i, j: (i, 0)),
        out_shape=jax.ShapeDtypeStruct((m, d), F32),
        scratch_shapes=[pltpu.VMEM((tm, nsel), F32), pltpu.VMEM((N_KEYS, tm, N_KEYS), BF16),
                        pltpu.VMEM((W_GROUP * N_KEYS, N_KEYS), F32),
                        pltpu.VMEM((W_GROUP * N_KEYS, N_KEYS), F32)],
        compiler_params=_params("parallel", "arbitrary"),
        name="peer_out",
    )(h, asel, gate, e1, e2, v, out_g.reshape(1, d))


def kernel(x, mem, norm_mix_g, w_in, conv_dw_w, conv_dw_b, conv_ln_g, conv_ln_b, sc_w, w_branch, b_gate, w_out, norm_xa_g, norm_mem_g, xa_wq, xa_wkv, xa_wo, norm_ffn_g, peer_wq, peer_keys, peer_u, peer_v, final_g):
    b, s, d = x.shape
    depth = w_in.shape[0]
    assert depth >= 1
    n_mem = mem.shape[1]
    sb_w = w_branch.shape[2]
    cv_w = conv_dw_w.shape[2]
    nb = w_branch.shape[1]
    qkv_w, conv_w = 3 * sb_w, 5 * cv_w
    m = b * s
    tm_big = min(1024, m)

    h = x.reshape(m, d)
    mem2 = mem.reshape(b * n_mem, d)
    for l in range(depth):
        w_in_l = w_in[l].astype(BF16)
        p_qkv = _norm_matmul(h, norm_mix_g[l], w_in_l, col0=0, ncols=qkv_w, out_dtype=BF16,
                             tm=tm_big, tn=512, name="in_proj_qkv")
        y_sb = _sb_attention(p_qkv.reshape(b, s, qkv_w), width=sb_w, bq=512, bk=256)
        y_cv, y_sc = _conv_branches(h.reshape(b, s, d), norm_mix_g[l], w_in_l, qkv_w, conv_dw_w[l],
                                    conv_dw_b[l], conv_ln_g[l], conv_ln_b[l], sc_w[l], ts=256)
        h = _merge(y_sb.reshape(m, sb_w), y_cv.reshape(m, cv_w), y_sc.reshape(m, cv_w), h,
                   norm_mix_g[l], w_in_l, qkv_w + conv_w, b_gate[l], w_branch[l].astype(BF16),
                   w_out[l].astype(BF16), tm=256)

        kv = _norm_matmul(mem2, norm_mem_g[l], xa_wkv[l].astype(BF16), col0=0, ncols=2 * d,
                          out_dtype=BF16, tm=min(1024, b * n_mem), tn=512, name="mem_kv")
        h = _cross_attention(h.reshape(b, s, d), norm_xa_g[l], xa_wq[l].astype(BF16),
                             kv.reshape(b, n_mem, 2 * d), xa_wo[l].astype(BF16), tm=256).reshape(m, d)

        keys = peer_keys[l].astype(BF16).reshape(PEER_HEADS * 2, N_KEYS, peer_keys.shape[-1])
        hn, e1, e2, gate = _peer_route(h, norm_ffn_g[l], peer_wq[l].astype(BF16), keys, tm=256)
        asel = _peer_asel(hn, peer_u[l].astype(BF16), e1, e2, tm=min(512, m),
                          te=peer_u.shape[1])
        h = _peer_out(h, asel, gate, e1, e2, peer_v[l].astype(BF16), final_g, tm=min(512, m),
                      te=4096, out_norm=(l == depth - 1))
    return h.reshape(b, s, d)
```
